```python
import jax, jax.numpy as jnp
from jax import lax
import numpy as np

D_MODEL = 1024
BATCH = 8
SEQ = 2048
DEPTH = 4
DEC_BATCH = 128
DEC_SEQ = 1
PAST_LEN = 16384
PAGE_SIZE = 128

SSD_EXPAND = 2
D_INNER = SSD_EXPAND * D_MODEL
SSD_HEAD_DIM = 64
SSD_HEADS = D_INNER // SSD_HEAD_DIM
SSD_GROUPS = 4
SSD_HPG = SSD_HEADS // SSD_GROUPS
SSD_STATE = 128
CONV_WIDTH = 4
CONV_DIM = D_INNER + 2 * SSD_GROUPS * SSD_STATE
CHUNK = 128
POOL_WINDOWS = (2, 4, 8, 16)
POOL_GROUPS = 4
POOL_WIDTH = D_MODEL
POOL_GW = POOL_WIDTH // POOL_GROUPS
POOL_BUF = 15
MEM_LEN = 256
X_HEADS = 4
X_HEAD_DIM = D_MODEL // X_HEADS
D_FF = 4 * D_MODEL
EPS = 1e-6
OFF_XBC = D_INNER
OFF_DT = OFF_XBC + CONV_DIM
OFF_POOL = OFF_DT + SSD_HEADS
OFF_GA = OFF_POOL + POOL_WIDTH
OFF_GB = OFF_GA + D_MODEL
IN_DIM = OFF_GB + D_MODEL

kernel_name = "hybrid_ssd_pool_gated_decoder_step"


def rmsnorm(x, g):
    xf = x.astype(jnp.float32)
    y = xf * lax.rsqrt(jnp.mean(xf * xf, axis=-1, keepdims=True) + EPS)
    return (y * g.astype(jnp.float32)).astype(x.dtype)


def ssd_scan(xs, dt, A, Bm, Cm, s0):
    b, L = xs.shape[0], xs.shape[1]
    lc = CHUNK if L % CHUNK == 0 else L
    nc = L // lc
    xs = xs.reshape(b, nc, lc, SSD_GROUPS, SSD_HPG, SSD_HEAD_DIM)
    dt = dt.reshape(b, nc, lc, SSD_GROUPS, SSD_HPG)
    Bm = Bm.reshape(b, nc, lc, SSD_GROUPS, SSD_STATE)
    Cm = Cm.reshape(b, nc, lc, SSD_GROUPS, SSD_STATE)
    cs = jnp.cumsum(dt * A, axis=2)
    seg = cs[:, :, :, None] - cs[:, :, None, :]
    causal = jnp.tril(jnp.ones((lc, lc), dtype=bool))[None, None, :, :, None, None]
    lmat = jnp.exp(jnp.where(causal, seg, -jnp.inf))
    xdt = xs * dt[..., None]
    cb = jnp.einsum('bclgn,bcsgn->bclsg', Cm, Bm)
    y_diag = jnp.einsum('bclsgk,bcsgkp->bclgkp', cb[..., None] * lmat, xdt)
    decay_to_end = jnp.exp(cs[:, :, -1:] - cs)
    chunk_states = jnp.einsum('bclgn,bclgkp->bcgkpn', Bm, xdt * decay_to_end[..., None])
    chunk_decay = jnp.exp(cs[:, :, -1])

    def step(s, inp):
        dec, st = inp
        return s * dec[..., None, None] + st, s

    s_fin, s_prev = lax.scan(step, s0, (jnp.moveaxis(chunk_decay, 1, 0), jnp.moveaxis(chunk_states, 1, 0)))
    s_prev = jnp.moveaxis(s_prev, 0, 1)
    y_off = jnp.einsum('bclgn,bcgkpn->bclgkp', Cm, s_prev) * jnp.exp(cs)[..., None]
    y = (y_diag + y_off).reshape(b, L, SSD_GROUPS, SSD_HPG, SSD_HEAD_DIM)
    return y, s_fin


def ssd_branch(z, xbc, dt_raw, conv_buf, ssm0, conv_w, conv_b, dt_bias, a_log, d_skip, norm_w):
    b, L, _ = xbc.shape
    cat = jnp.concatenate([conv_buf.astype(xbc.dtype), xbc], axis=1)
    new_conv = cat[:, -(CONV_WIDTH - 1):]
    conv = conv_b + cat[:, 0:L] * conv_w[0]
    for k in range(1, CONV_WIDTH):
        conv = conv + cat[:, k:k + L] * conv_w[k]
    act = jax.nn.silu(conv.astype(jnp.float32))
    xs = act[..., :D_INNER].reshape(b, L, SSD_GROUPS, SSD_HPG, SSD_HEAD_DIM)
    Bm = act[..., D_INNER:D_INNER + SSD_GROUPS * SSD_STATE].reshape(b, L, SSD_GROUPS, SSD_STATE)
    Cm = act[..., D_INNER + SSD_GROUPS * SSD_STATE:].reshape(b, L, SSD_GROUPS, SSD_STATE)
    dt = jax.nn.softplus(dt_raw.astype(jnp.float32) + dt_bias.astype(jnp.float32)).reshape(b, L, SSD_GROUPS, SSD_HPG)
    A = -jnp.exp(a_log.astype(jnp.float32)).reshape(SSD_GROUPS, SSD_HPG)
    s0 = ssm0.astype(jnp.float32).reshape(b, SSD_GROUPS, SSD_HPG, SSD_HEAD_DIM, SSD_STATE)
    y, s_fin = ssd_scan(xs, dt, A, Bm, Cm, s0)
    y = y + d_skip.astype(jnp.float32).reshape(SSD_GROUPS, SSD_HPG)[:, :, None] * xs
    y = y.reshape(b, L, D_INNER) * jax.nn.silu(z.astype(jnp.float32))
    yg = y.reshape(b, L, SSD_GROUPS, D_INNER // SSD_GROUPS)
    yg = yg * lax.rsqrt(jnp.mean(yg * yg, axis=-1, keepdims=True) + EPS)
    y = yg.reshape(b, L, D_INNER) * norm_w.astype(jnp.float32)
    new_ssm = s_fin.reshape(b, SSD_HEADS, SSD_HEAD_DIM, SSD_STATE)
    return y.astype(z.dtype), new_conv, new_ssm.astype(z.dtype)


def pool_branch(u, pool_buf, start_pos, pool_w, pool_scale):
    b, L, _ = u.shape
    cat = jnp.concatenate([pool_buf.astype(u.dtype), u], axis=1)
    new_buf = cat[:, -POOL_BUF:]
    cs = jnp.cumsum(cat.astype(jnp.float32), axis=1)
    cs = jnp.concatenate([jnp.zeros((b, 1, POOL_WIDTH), jnp.float32), cs], axis=1)
    end = cs[:, POOL_BUF + 1:]
    pos = start_pos + jnp.arange(L, dtype=jnp.int32)
    parts = []
    for gi, w in enumerate(POOL_WINDOWS):
        lo, hi = gi * POOL_GW, (gi + 1) * POOL_GW
        wsum = end[..., lo:hi] - cs[:, POOL_BUF + 1 - w:POOL_BUF + 1 - w + L, lo:hi]
        cnt = jnp.minimum(w, pos + 1).astype(jnp.float32)[None, :, None]
        parts.append(wsum / cnt)
    mean = jnp.concatenate(parts, axis=-1)
    d = (mean - u.astype(jnp.float32)).reshape(b, L, POOL_GROUPS, POOL_GW)
    out = jnp.einsum('blgc,gcd->blgd', d, pool_w.astype(jnp.float32)).reshape(b, L, POOL_WIDTH)
    out = out * pool_scale.astype(jnp.float32)
    return out.astype(u.dtype), new_buf


def cross_attn(h, mem_k, mem_v, w_xq, w_xo):
    b, L, _ = h.shape
    q = (h @ w_xq).reshape(b, L, X_HEADS, X_HEAD_DIM)
    s = jnp.einsum('blhd,bmhd->bhlm', q.astype(jnp.float32), mem_k.astype(jnp.float32)) * (X_HEAD_DIM ** -0.5)
    pr = jax.nn.softmax(s, axis=-1)
    o = jnp.einsum('bhlm,bmhd->blhd', pr, mem_v.astype(jnp.float32)).reshape(b, L, D_MODEL)
    return o.astype(h.dtype) @ w_xo


def layer(x, conv_buf, ssm0, pool_buf, mem_k, mem_v, start_pos, p):
    u = rmsnorm(x, p['norm_mix'])
    proj = u @ p['w_in']
    z = proj[..., :OFF_XBC]
    xbc = proj[..., OFF_XBC:OFF_DT]
    dt_raw = proj[..., OFF_DT:OFF_POOL]
    pu = proj[..., OFF_POOL:OFF_GA]
    ga = proj[..., OFF_GA:OFF_GB]
    gb = proj[..., OFF_GB:]
    ya, new_conv, new_ssm = ssd_branch(z, xbc, dt_raw, conv_buf, ssm0, p['conv_w'], p['conv_b'],
                                       p['dt_bias'], p['a_log'], p['d_skip'], p['ssd_norm'])
    yb, new_pool = pool_branch(pu, pool_buf, start_pos, p['pool_w'], p['pool_scale'])
    merged = jax.nn.sigmoid(ga) * (ya @ p['w_ssd_proj']) + jax.nn.sigmoid(gb) * yb
    x = x + merged @ p['w_o']
    x = x + cross_attn(rmsnorm(x, p['norm_x']), mem_k, mem_v, p['w_xq'], p['w_xo'])
    hmid = rmsnorm(x, p['norm_mlp']) @ p['w_up']
    x = x + jnp.square(jax.nn.relu(hmid)) @ p['w_down']
    return x, new_conv, new_ssm, new_pool


def setup_inputs(seed: int = 0) -> dict:
    key = jax.random.key(seed)
    ks = jax.random.split(key, 40)
    f32 = jnp.float32
    nrm = lambda k, shape, scale: jax.random.normal(k, shape, f32) * scale
    dt0 = jnp.exp(jax.random.uniform(ks[10], (DEPTH, SSD_HEADS), f32) * (np.log(0.1) - np.log(0.001)) + np.log(0.001))
    dt_bias = dt0 + jnp.log(-jnp.expm1(-dt0))
    a_log = jnp.log(jax.random.uniform(ks[11], (DEPTH, SSD_HEADS), f32, 1.0, 16.0))
    return {
        'x_prompt': nrm(ks[0], (BATCH, SEQ, D_MODEL), 1.0),
        'x_sample': nrm(ks[1], (DEC_BATCH, DEC_SEQ, D_MODEL), 1.0),
        'mem_prompt': nrm(ks[2], (BATCH, MEM_LEN, D_MODEL), 1.0),
        'state_ssm': nrm(ks[3], (DEPTH, DEC_BATCH, SSD_HEADS, SSD_HEAD_DIM, SSD_STATE), 0.1),
        'state_conv': nrm(ks[4], (DEPTH, DEC_BATCH, CONV_WIDTH - 1, CONV_DIM), 1.0),
        'state_pool': nrm(ks[5], (DEPTH, DEC_BATCH, POOL_BUF, POOL_WIDTH), 1.0),
        'cache_mem_k': nrm(ks[6], (DEPTH, DEC_BATCH, MEM_LEN, X_HEADS, X_HEAD_DIM), 1.0),
        'cache_mem_v': nrm(ks[7], (DEPTH, DEC_BATCH, MEM_LEN, X_HEADS, X_HEAD_DIM), 1.0),
        'norm_mix': 1.0 + nrm(ks[8], (DEPTH, D_MODEL), 0.02),
        'w_in': nrm(ks[9], (DEPTH, D_MODEL, IN_DIM), D_MODEL ** -0.5),
        'conv_w': nrm(ks[12], (DEPTH, CONV_WIDTH, CONV_DIM), CONV_WIDTH ** -0.5),
        'conv_b': nrm(ks[13], (DEPTH, CONV_DIM), 0.01),
        'dt_bias': dt_bias,
        'a_log': a_log,
        'd_skip': 1.0 + nrm(ks[14], (DEPTH, SSD_HEADS), 0.02),
        'ssd_norm': 1.0 + nrm(ks[15], (DEPTH, D_INNER), 0.02),
        'w_ssd_proj': nrm(ks[16], (DEPTH, D_INNER, D_MODEL), D_INNER ** -0.5),
        'pool_w': nrm(ks[17], (DEPTH, POOL_GROUPS, POOL_GW, POOL_GW), POOL_GW ** -0.5),
        'pool_scale': 1.0 + nrm(ks[18], (DEPTH, POOL_WIDTH), 0.1),
        'w_o': nrm(ks[19], (DEPTH, D_MODEL, D_MODEL), D_MODEL ** -0.5),
        'norm_x': 1.0 + nrm(ks[20], (DEPTH, D_MODEL), 0.02),
        'w_xq': nrm(ks[21], (DEPTH, D_MODEL, D_MODEL), D_MODEL ** -0.5),
        'w_xk': nrm(ks[22], (DEPTH, D_MODEL, D_MODEL), D_MODEL ** -0.5),
        'w_xv': nrm(ks[23], (DEPTH, D_MODEL, D_MODEL), D_MODEL ** -0.5),
        'w_xo': nrm(ks[24], (DEPTH, D_MODEL, D_MODEL), D_MODEL ** -0.5),
        'norm_mlp': 1.0 + nrm(ks[25], (DEPTH, D_MODEL), 0.02),
        'w_up': nrm(ks[26], (DEPTH, D_MODEL, D_FF), D_MODEL ** -0.5),
        'w_down': nrm(ks[27], (DEPTH, D_FF, D_MODEL), D_FF ** -0.5),
        'norm_final': 1.0 + nrm(ks[28], (D_MODEL,), 0.02),
    }


def reference(x_prompt, x_sample, mem_prompt, state_ssm, state_conv, state_pool, cache_mem_k, cache_mem_v,
              norm_mix, w_in, conv_w, conv_b, dt_bias, a_log, d_skip, ssd_norm, w_ssd_proj, pool_w, pool_scale,
              w_o, norm_x, w_xq, w_xk, w_xv, w_xo, norm_mlp, w_up, w_down, norm_final):
    bp = x_prompt.shape[0]
    dtp = x_prompt.dtype
    hp, hs = x_prompt, x_sample
    ssm_p, conv_p, pool_p, mk_p, mv_p = [], [], [], [], []
    ssm_s, conv_s, pool_s = [], [], []
    zero_conv = jnp.zeros((bp, CONV_WIDTH - 1, CONV_DIM), dtp)
    zero_ssm = jnp.zeros((bp, SSD_HEADS, SSD_HEAD_DIM, SSD_STATE), dtp)
    zero_pool = jnp.zeros((bp, POOL_BUF, POOL_WIDTH), dtp)
    for l in range(DEPTH):
        p = dict(norm_mix=norm_mix[l], w_in=w_in[l], conv_w=conv_w[l], conv_b=conv_b[l], dt_bias=dt_bias[l],
                 a_log=a_log[l], d_skip=d_skip[l], ssd_norm=ssd_norm[l], w_ssd_proj=w_ssd_proj[l],
                 pool_w=pool_w[l], pool_scale=pool_scale[l], w_o=w_o[l], norm_x=norm_x[l], w_xq=w_xq[l],
                 w_xo=w_xo[l], norm_mlp=norm_mlp[l], w_up=w_up[l], w_down=w_down[l])
        mk = (mem_prompt @ w_xk[l]).reshape(bp, MEM_LEN, X_HEADS, X_HEAD_DIM)
        mv = (mem_prompt @ w_xv[l]).reshape(bp, MEM_LEN, X_HEADS, X_HEAD_DIM)
        hp, nc, ns, npool = layer(hp, zero_conv, zero_ssm, zero_pool, mk, mv, 0, p)
        ssm_p.append(ns); conv_p.append(nc); pool_p.append(npool); mk_p.append(mk); mv_p.append(mv)
        hs, nc, ns, npool = layer(hs, state_conv[l], state_ssm[l], state_pool[l],
                                  cache_mem_k[l], cache_mem_v[l], PAST_LEN, p)
        ssm_s.append(ns); conv_s.append(nc); pool_s.append(npool)
    y_prompt = rmsnorm(hp, norm_final)
    y_sample = rmsnorm(hs, norm_final)
    return (y_prompt, y_sample,
            jnp.stack(ssm_p), jnp.stack(conv_p), jnp.stack(pool_p), jnp.stack(mk_p), jnp.stack(mv_p),
            jnp.stack(ssm_s), jnp.stack(conv_s), jnp.stack(pool_s))
```

```python
import functools

import jax
import jax.numpy as jnp
from jax import lax
from jax.experimental import pallas as pl
from jax.experimental.pallas import tpu as pltpu

F32 = jnp.float32
BF16 = jnp.bfloat16

D_MODEL = 1024
D_INNER = 2048
SSD_HEAD_DIM = 64
SSD_HEADS = 32
SSD_GROUPS = 4
SSD_STATE = 128
GROUP_W = D_INNER // SSD_GROUPS
CONV_WIDTH = 4
CONV_DIM = D_INNER + 2 * SSD_GROUPS * SSD_STATE
CHUNK = 128
POOL_WINDOWS = (2, 4, 8, 16)
POOL_GW = 256
POOL_BUF = 15
POOL_HIST = 16
MEM_LEN = 256
X_HEADS = 4
X_HEAD_DIM = 256
D_FF = 4096
EPS = 1e-6
PAST_LEN = 16384
OFF_XBC = D_INNER
OFF_DT = OFF_XBC + CONV_DIM
OFF_POOL = OFF_DT + SSD_HEADS
LANES = 128
DT_PAD = 256
PGD_W = 3 * D_MODEL + DT_PAD
DT_BLK = 3 * D_MODEL // LANES
VMEM_LIMIT = 52 * 1024 * 1024


def _cparams(*sem):
    return pltpu.CompilerParams(dimension_semantics=sem, vmem_limit_bytes=VMEM_LIMIT)


def _dot(a, b):
    return jnp.dot(a, b, preferred_element_type=F32)


def _dot_nt(a, b):
    return lax.dot_general(a, b, (((1,), (1,)), ((), ())), preferred_element_type=F32)


def _split3(x):
    hi = x.astype(BF16)
    r1 = x - hi.astype(F32)
    mid = r1.astype(BF16)
    lo = (r1 - mid.astype(F32)).astype(BF16)
    return hi, mid, lo


def _silu(x):
    return x * jax.nn.sigmoid(x)


def _softplus(x):
    return jnp.maximum(x, 0.0) + jnp.log1p(jnp.exp(-jnp.abs(x)))


def _mm_body(*refs, norm, act, has_res, out_scale):
    it = iter(refs)
    x_ref = next(it)
    w_ref = next(it)
    g_ref = next(it) if norm else None
    r_ref = next(it) if has_res else None
    o_ref = next(it)
    u_ref = next(it)

    @pl.when(pl.program_id(1) == 0)
    def _():
        x = x_ref[...].astype(F32)
        if norm:
            x = x * lax.rsqrt(jnp.mean(x * x, axis=-1, keepdims=True) + EPS) * g_ref[...]
        u_ref[...] = x.astype(BF16)

    acc = _dot(u_ref[...], w_ref[...])
    if act == "relu2":
        acc = jnp.square(jnp.maximum(acc, 0.0))
    if out_scale != 1.0:
        acc = acc * out_scale
    if has_res:
        acc = acc + r_ref[...]
    o_ref[...] = acc.astype(o_ref.dtype)


def _mm(x, w, *, g=None, res=None, act=None, out_scale=1.0, out_dtype=F32, tm=512, tn=1024):
    m, k = x.shape
    n = w.shape[1]
    tm = min(tm, m)
    tn = min(tn, n)
    assert m % tm == 0 and n % tn == 0, (m, n, tm, tn)
    in_specs = [pl.BlockSpec((tm, k), lambda i, j: (i, 0)),
                pl.BlockSpec((k, tn), lambda i, j: (0, j))]
    args = [x, w]
    if g is not None:
        in_specs.append(pl.BlockSpec((1, k), lambda i, j: (0, 0)))
        args.append(g.reshape(1, k))
    if res is not None:
        in_specs.append(pl.BlockSpec((tm, tn), lambda i, j: (i, j)))
        args.append(res)
    return pl.pallas_call(
        functools.partial(_mm_body, norm=g is not None, act=act, has_res=res is not None, out_scale=out_scale),
        grid=(m // tm, n // tn),
        in_specs=in_specs,
        out_specs=pl.BlockSpec((tm, tn), lambda i, j: (i, j)),
        out_shape=jax.ShapeDtypeStruct((m, n), out_dtype),
        scratch_shapes=[pltpu.VMEM((tm, k), BF16)],
        compiler_params=_cparams("parallel", "arbitrary"),
        name="mm",
    )(*args)


def _rmsnorm_body(x_ref, g_ref, o_ref):
    x = x_ref[...]
    o_ref[...] = x * lax.rsqrt(jnp.mean(x * x, axis=-1, keepdims=True) + EPS) * g_ref[...]


def _rmsnorm(x, g, tm=512):
    m, k = x.shape
    tm = min(tm, m)
    return pl.pallas_call(
        _rmsnorm_body,
        grid=(m // tm,),
        in_specs=[pl.BlockSpec((tm, k), lambda i: (i, 0)), pl.BlockSpec((1, k), lambda i: (0, 0))],
        out_specs=pl.BlockSpec((tm, k), lambda i: (i, 0)),
        out_shape=jax.ShapeDtypeStruct((m, k), F32),
        compiler_params=_cparams("parallel"),
        name="rmsnorm",
    )(x, g.reshape(1, k))


def _ssd_body(xbc_ref, z_ref, dtr_ref, cw_ref, cb_ref, dtb_ref, a_ref, de_ref, nw_ref,
              ya_ref, sst_ref, cst_ref, cbuf, st_scr, *, nchunks):
    c = pl.program_id(1)
    L = CHUNK

    @pl.when(c == 0)
    def _():
        cbuf[0:8, :] = jnp.zeros((8, CONV_DIM), F32)
        st_scr[...] = jnp.zeros_like(st_scr)

    xbc = xbc_ref[...]
    cbuf[8:8 + L, :] = xbc
    conv = cb_ref[...] + cbuf[5:5 + L, :] * cw_ref[0:1, :]
    conv = conv + cbuf[6:6 + L, :] * cw_ref[1:2, :]
    conv = conv + cbuf[7:7 + L, :] * cw_ref[2:3, :]
    conv = conv + xbc * cw_ref[3:4, :]
    cbuf[0:8, :] = xbc[L - 8:L, :]
    act = _silu(conv)
    xs = act[:, :D_INNER]

    dt = _softplus(dtr_ref[...] + dtb_ref[...])
    a = dt * a_ref[...]
    row = lax.broadcasted_iota(jnp.int32, (L, L), 0)
    col = lax.broadcasted_iota(jnp.int32, (L, L), 1)
    tri = row >= col
    ltri = tri.astype(BF16)
    a_hi, a_mid, a_lo = _split3(a)
    cs = _dot(ltri, a_hi) + _dot(ltri, a_mid) + _dot(ltri, a_lo)
    cs_t = cs.T
    dt_t = dt.T
    cs_last = cs[L - 1:L, :]
    wdec = dt * jnp.exp(cs_last - cs)
    lt64 = col < SSD_HEAD_DIM
    lt64_row = lax.broadcasted_iota(jnp.int32, (1, L), 1) < SSD_HEAD_DIM

    y_pairs = []
    for g in range(SSD_GROUPS):
        bm = act[:, D_INNER + g * SSD_STATE:D_INNER + (g + 1) * SSD_STATE]
        cm = act[:, D_INNER + (SSD_GROUPS + g) * SSD_STATE:D_INNER + (SSD_GROUPS + g + 1) * SSD_STATE]
        bm_b = bm.astype(BF16)
        cb = _dot_nt(cm.astype(BF16), bm_b)
        xw_pairs = []
        dec_pairs = []
        for kk in range(4):
            k = g * 4 + kk
            lhs_m, lhs_c, colbs = [], [], []
            for h in (2 * k, 2 * k + 1):
                colb = jnp.broadcast_to(cs[:, h:h + 1], (L, L))
                seg = colb - cs_t[h:h + 1, :]
                lm = jnp.where(tri, jnp.exp(seg), 0.0)
                lhs_m.append(cb * lm * dt_t[h:h + 1, :])
                lhs_c.append(cm * jnp.exp(colb))
                colbs.append(colb)
            lhs = jnp.concatenate(lhs_m + lhs_c, axis=1).astype(BF16)
            xs_p = xs[:, k * LANES:(k + 1) * LANES]
            st_p = st_scr[:, k * LANES:(k + 1) * LANES]
            rhs = jnp.concatenate([jnp.where(lt64, xs_p, 0.0), jnp.where(lt64, 0.0, xs_p),
                                   jnp.where(lt64, st_p, 0.0), jnp.where(lt64, 0.0, st_p)],
                                  axis=0).astype(BF16)
            y_pairs.append(_dot(lhs, rhs))
            wb = jnp.where(lt64, jnp.broadcast_to(wdec[:, 2 * k:2 * k + 1], (L, L)),
                           jnp.broadcast_to(wdec[:, 2 * k + 1:2 * k + 2], (L, L)))
            xw_pairs.append(xs_p * wb)
            dec_pairs.append(jnp.exp(jnp.where(lt64_row, colbs[0][L - 1:L, :], colbs[1][L - 1:L, :])))
        xw = jnp.concatenate(xw_pairs, axis=1).astype(BF16)
        dec = jnp.concatenate(dec_pairs, axis=1)
        inc = _dot(bm.T.astype(BF16), xw)
        sl = slice(g * GROUP_W, (g + 1) * GROUP_W)
        st_scr[:, sl] = st_scr[:, sl] * dec + inc

    y = jnp.concatenate(y_pairs, axis=1)
    y = y + de_ref[...] * xs
    y = y * _silu(z_ref[...])
    parts = []
    for g in range(SSD_GROUPS):
        yg = y[:, g * GROUP_W:(g + 1) * GROUP_W]
        parts.append(yg * lax.rsqrt(jnp.mean(yg * yg, axis=-1, keepdims=True) + EPS))
    ya_ref[...] = (jnp.concatenate(parts, axis=1) * nw_ref[...]).astype(ya_ref.dtype)

    @pl.when(c == nchunks - 1)
    def _():
        for k in range(D_INNER // LANES):
            sst_ref[0, k * LANES:(k + 1) * LANES, :] = st_scr[:, k * LANES:(k + 1) * LANES].T
        cst_ref[0] = cbuf[8 - (CONV_WIDTH - 1):8, :]


def _ssd_prompt(xbc, z, pgd, cw, cb, dtb, a_neg, d_e, nw, batch, seq):
    nchunks = seq // CHUNK
    tok = lambda b, c: (b * nchunks + c, 0)
    const = lambda b, c: (0, 0)
    return pl.pallas_call(
        functools.partial(_ssd_body, nchunks=nchunks),
        grid=(batch, nchunks),
        in_specs=[pl.BlockSpec((CHUNK, CONV_DIM), tok),
                  pl.BlockSpec((CHUNK, D_INNER), tok),
                  pl.BlockSpec((CHUNK, LANES), lambda b, c: (b * nchunks + c, DT_BLK)),
                  pl.BlockSpec((CONV_WIDTH, CONV_DIM), const),
                  pl.BlockSpec((1, CONV_DIM), const),
                  pl.BlockSpec((1, LANES), const),
                  pl.BlockSpec((1, LANES), const),
                  pl.BlockSpec((1, D_INNER), const),
                  pl.BlockSpec((1, D_INNER), const)],
        out_specs=[pl.BlockSpec((CHUNK, D_INNER), tok),
                   pl.BlockSpec((1, D_INNER, SSD_STATE), lambda b, c: (b, 0, 0)),
                   pl.BlockSpec((1, CONV_WIDTH - 1, CONV_DIM), lambda b, c: (b, 0, 0))],
        out_shape=[jax.ShapeDtypeStruct((batch * seq, D_INNER), BF16),
                   jax.ShapeDtypeStruct((batch, D_INNER, SSD_STATE), F32),
                   jax.ShapeDtypeStruct((batch, CONV_WIDTH - 1, CONV_DIM), F32)],
        scratch_shapes=[pltpu.VMEM((8 + CHUNK, CONV_DIM), F32),
                        pltpu.VMEM((SSD_STATE, D_INNER), F32)],
        compiler_params=_cparams("parallel", "arbitrary"),
        name="ssd_prompt",
    )(xbc, z, pgd, cw, cb, dtb, a_neg, d_e, nw)


def _pool_body(pu_ref, d_ref, pst_ref, pbuf, *, tt, ntiles):
    c = pl.program_id(1)

    @pl.when(c == 0)
    def _():
        pbuf[0:POOL_HIST, :] = jnp.zeros((POOL_HIST, D_MODEL), F32)

    pu = pu_ref[...]
    pbuf[POOL_HIST:POOL_HIST + tt, :] = pu
    pos = c * tt + lax.broadcasted_iota(jnp.int32, (tt, 1), 0)
    outs = []
    for gi, w in enumerate(POOL_WINDOWS):
        lo = gi * POOL_GW
        cur = pu[:, lo:lo + POOL_GW]
        acc = cur
        for k in range(1, w):
            acc = acc + pbuf[POOL_HIST - k:POOL_HIST - k + tt, lo:lo + POOL_GW]
        cnt = jnp.minimum(w, pos + 1).astype(F32)
        outs.append(acc / cnt - cur)
    d_ref[...] = jnp.concatenate(outs, axis=1).astype(d_ref.dtype)

    @pl.when(c == ntiles - 1)
    def _():
        pst_ref[0] = pbuf[tt + 1:tt + POOL_HIST, :]

    pbuf[0:POOL_HIST, :] = pbuf[tt:tt + POOL_HIST, :]


def _pool_prompt(pgd, batch, seq, tt=256):
    tt = min(tt, seq)
    assert seq % tt == 0
    ntiles = seq // tt
    return pl.pallas_call(
        functools.partial(_pool_body, tt=tt, ntiles=ntiles),
        grid=(batch, ntiles),
        in_specs=[pl.BlockSpec((tt, D_MODEL), lambda b, c: (b * ntiles + c, 0))],
        out_specs=[pl.BlockSpec((tt, D_MODEL), lambda b, c: (b * ntiles + c, 0)),
                   pl.BlockSpec((1, POOL_BUF, D_MODEL), lambda b, c: (b, 0, 0))],
        out_shape=[jax.ShapeDtypeStruct((batch * seq, D_MODEL), BF16),
                   jax.ShapeDtypeStruct((batch, POOL_BUF, D_MODEL), F32)],
        scratch_shapes=[pltpu.VMEM((POOL_HIST + tt, D_MODEL), F32)],
        compiler_params=_cparams("parallel", "arbitrary"),
        name="pool_prompt",
    )(pgd)


def _mix_body(ya_ref, d_ref, ga_ref, gb_ref, x_ref, wsp_ref, pw_ref, ps_ref, wo_ref, o_ref):
    ssd = _dot(ya_ref[...], wsp_ref[...])
    d = d_ref[...]
    pool = jnp.concatenate(
        [_dot(d[:, g * POOL_GW:(g + 1) * POOL_GW], pw_ref[g]) for g in range(len(POOL_WINDOWS))], axis=1)
    pool = pool * ps_ref[...]
    merged = jax.nn.sigmoid(ga_ref[...]) * ssd + jax.nn.sigmoid(gb_ref[...]) * pool
    o_ref[...] = x_ref[...] + _dot(merged.astype(BF16), wo_ref[...])


def _mix(ya, d, pgd, x, wsp, pw, ps, wo, tm=512):
    m = x.shape[0]
    tm = min(tm, m)
    tok = lambda i: (i, 0)
    const2 = lambda i: (0, 0)
    return pl.pallas_call(
        _mix_body,
        grid=(m // tm,),
        in_specs=[pl.BlockSpec((tm, D_INNER), tok),
                  pl.BlockSpec((tm, D_MODEL), tok),
                  pl.BlockSpec((tm, D_MODEL), lambda i: (i, 1)),
                  pl.BlockSpec((tm, D_MODEL), lambda i: (i, 2)),
                  pl.BlockSpec((tm, D_MODEL), tok),
                  pl.BlockSpec((D_INNER, D_MODEL), const2),
                  pl.BlockSpec((len(POOL_WINDOWS), POOL_GW, POOL_GW), lambda i: (0, 0, 0)),
                  pl.BlockSpec((1, D_MODEL), const2),
                  pl.BlockSpec((D_MODEL, D_MODEL), const2)],
        out_specs=pl.BlockSpec((tm, D_MODEL), tok),
        out_shape=jax.ShapeDtypeStruct((m, D_MODEL), F32),
        compiler_params=_cparams("parallel"),
        name="mix",
    )(ya, d, pgd, pgd, x, wsp, pw, ps, wo)


def _attn_body(q_ref, k_ref, v_ref, o_ref):
    for h in range(X_HEADS):
        sl = slice(h * X_HEAD_DIM, (h + 1) * X_HEAD_DIM)
        s = _dot_nt(q_ref[:, sl], k_ref[:, sl].astype(BF16))
        p = jnp.exp(s - jnp.max(s, axis=-1, keepdims=True))
        p = p / jnp.sum(p, axis=-1, keepdims=True)
        o_ref[:, sl] = _dot(p.astype(BF16), v_ref[:, sl].astype(BF16)).astype(o_ref.dtype)


def _attn_prompt(q, mk, mv, batch, seq, tm=512):
    tm = min(tm, seq)
    assert seq % tm == 0
    nt = seq // tm
    return pl.pallas_call(
        _attn_body,
        grid=(batch, nt),
        in_specs=[pl.BlockSpec((tm, D_MODEL), lambda b, i: (b * nt + i, 0)),
                  pl.BlockSpec((MEM_LEN, D_MODEL), lambda b, i: (b, 0)),
                  pl.BlockSpec((MEM_LEN, D_MODEL), lambda b, i: (b, 0))],
        out_specs=pl.BlockSpec((tm, D_MODEL), lambda b, i: (b * nt + i, 0)),
        out_shape=jax.ShapeDtypeStruct((batch * seq, D_MODEL), BF16),
        compiler_params=_cparams("parallel", "parallel"),
        name="attn_prompt",
    )(q, mk, mv)


STEP_TB = 32


def _step_body(xbc_ref, cst_ref, cw_ref, cb_ref, dtr_ref, dtb_ref, a_ref, pu_ref, pst_ref,
               ncst_ref, npst_ref, act_ref, xdt_ref, dec_ref, d_ref, *, start_pos):
    xbc = xbc_ref[...]
    conv = cb_ref[...]
    for k in range(CONV_WIDTH - 1):
        conv = conv + cst_ref[:, k * CONV_DIM:(k + 1) * CONV_DIM] * cw_ref[k:k + 1, :]
    conv = conv + xbc * cw_ref[CONV_WIDTH - 1:CONV_WIDTH, :]
    for k in range(CONV_WIDTH - 2):
        ncst_ref[:, k * CONV_DIM:(k + 1) * CONV_DIM] = cst_ref[:, (k + 1) * CONV_DIM:(k + 2) * CONV_DIM]
    ncst_ref[:, (CONV_WIDTH - 2) * CONV_DIM:] = xbc
    act = _silu(conv)
    act_ref[...] = act

    dt = _softplus(dtr_ref[...] + dtb_ref[...])
    dec = jnp.exp(dt * a_ref[...])
    col = lax.broadcasted_iota(jnp.int32, (1, LANES), 1)
    lt64 = col < SSD_HEAD_DIM
    m = dt.shape[0]
    for k in range(D_INNER // LANES):
        dt_e = jnp.where(lt64, jnp.broadcast_to(dt[:, 2 * k:2 * k + 1], (m, LANES)),
                         jnp.broadcast_to(dt[:, 2 * k + 1:2 * k + 2], (m, LANES)))
        dec_e = jnp.where(lt64, jnp.broadcast_to(dec[:, 2 * k:2 * k + 1], (m, LANES)),
                          jnp.broadcast_to(dec[:, 2 * k + 1:2 * k + 2], (m, LANES)))
        xdt_ref[:, k * LANES:(k + 1) * LANES] = act[:, k * LANES:(k + 1) * LANES] * dt_e
        dec_ref[:, k * LANES:(k + 1) * LANES] = dec_e

    pu = pu_ref[...]
    outs = []
    for gi, w in enumerate(POOL_WINDOWS):
        lo = gi * POOL_GW
        cur = pu[:, lo:lo + POOL_GW]
        acc = cur
        for k in range(1, w):
            j = POOL_BUF - k
            acc = acc + pst_ref[:, j * D_MODEL + lo:j * D_MODEL + lo + POOL_GW]
        cnt = float(min(w, start_pos + 1))
        outs.append(acc / cnt - cur)
    d_ref[...] = jnp.concatenate(outs, axis=1).astype(d_ref.dtype)
    npst_ref[:, :(POOL_BUF - 1) * D_MODEL] = pst_ref[:, D_MODEL:]
    npst_ref[:, (POOL_BUF - 1) * D_MODEL:] = pu


def _sample_step(xbc, cst, cw, cb, pgd, dtb, a_neg, pst, start_pos):
    m = xbc.shape[0]
    tm = min(STEP_TB, m)
    full = lambda *shape: pl.BlockSpec(shape, lambda i: (0,) * len(shape))
    rows = lambda width: pl.BlockSpec((tm, width), lambda i: (i, 0))
    return pl.pallas_call(
        functools.partial(_step_body, start_pos=start_pos),
        grid=(m // tm,),
        in_specs=[rows(CONV_DIM), rows((CONV_WIDTH - 1) * CONV_DIM), full(CONV_WIDTH, CONV_DIM),
                  full(1, CONV_DIM), pl.BlockSpec((tm, LANES), lambda i: (i, DT_BLK)), full(1, LANES),
                  full(1, LANES), rows(D_MODEL), rows(POOL_BUF * D_MODEL)],
        out_specs=[rows((CONV_WIDTH - 1) * CONV_DIM), rows(POOL_BUF * D_MODEL), rows(CONV_DIM),
                   rows(D_INNER), rows(D_INNER), rows(D_MODEL)],
        out_shape=[jax.ShapeDtypeStruct((m, (CONV_WIDTH - 1) * CONV_DIM), F32),
                   jax.ShapeDtypeStruct((m, POOL_BUF * D_MODEL), F32),
                   jax.ShapeDtypeStruct((m, CONV_DIM), F32),
                   jax.ShapeDtypeStruct((m, D_INNER), F32),
                   jax.ShapeDtypeStruct((m, D_INNER), F32),
                   jax.ShapeDtypeStruct((m, D_MODEL), BF16)],
        compiler_params=_cparams("parallel"),
        name="sample_step",
    )(xbc, cst, cw, cb, pgd, dtb, a_neg, pgd, pst)


SSM_TB = 8


def _ssm_body(s_ref, xdt_ref, dec_ref, act_ref, so_ref, y_ref):
    for j in range(SSM_TB):
        for g in range(SSD_GROUPS):
            rows = slice(g * GROUP_W, (g + 1) * GROUP_W)
            b_row = act_ref[j:j + 1, D_INNER + g * SSD_STATE:D_INNER + (g + 1) * SSD_STATE]
            c_row = act_ref[j:j + 1, D_INNER + (SSD_GROUPS + g) * SSD_STATE:
                            D_INNER + (SSD_GROUPS + g + 1) * SSD_STATE]
            sn = s_ref[j, rows, :] * dec_ref[0, rows, j:j + 1] + xdt_ref[0, rows, j:j + 1] * b_row
            so_ref[j, rows, :] = sn
            y_ref[0, rows, j:j + 1] = jnp.sum(sn * c_row, axis=-1, keepdims=True)


def _sample_ssm(state, xdt_c, dec_c, act):
    m = state.shape[0]
    nb = m // SSM_TB
    return pl.pallas_call(
        _ssm_body,
        grid=(nb,),
        in_specs=[pl.BlockSpec((SSM_TB, D_INNER, SSD_STATE), lambda i: (i, 0, 0)),
                  pl.BlockSpec((1, D_INNER, SSM_TB), lambda i: (i, 0, 0)),
                  pl.BlockSpec((1, D_INNER, SSM_TB), lambda i: (i, 0, 0)),
                  pl.BlockSpec((SSM_TB, CONV_DIM), lambda i: (i, 0))],
        out_specs=[pl.BlockSpec((SSM_TB, D_INNER, SSD_STATE), lambda i: (i, 0, 0)),
                   pl.BlockSpec((1, D_INNER, SSM_TB), lambda i: (i, 0, 0))],
        out_shape=[jax.ShapeDtypeStruct((m, D_INNER, SSD_STATE), F32),
                   jax.ShapeDtypeStruct((nb, D_INNER, SSM_TB), F32)],
        compiler_params=_cparams("parallel"),
        name="sample_ssm",
    )(state, xdt_c, dec_c, act)


def _gate_body(y_ref, act_ref, z_ref, de_ref, nw_ref, o_ref):
    y = y_ref[...] + de_ref[...] * act_ref[:, :D_INNER]
    y = y * _silu(z_ref[...])
    parts = []
    for g in range(SSD_GROUPS):
        yg = y[:, g * GROUP_W:(g + 1) * GROUP_W]
        parts.append(yg * lax.rsqrt(jnp.mean(yg * yg, axis=-1, keepdims=True) + EPS))
    o_ref[...] = (jnp.concatenate(parts, axis=1) * nw_ref[...]).astype(o_ref.dtype)


def _sample_gate(y, act, z, d_e, nw):
    m = y.shape[0]
    full = lambda *shape: pl.BlockSpec(shape, lambda i: (0,) * len(shape))
    return pl.pallas_call(
        _gate_body,
        grid=(1,),
        in_specs=[full(m, D_INNER), full(m, CONV_DIM), full(m, D_INNER), full(1, D_INNER), full(1, D_INNER)],
        out_specs=full(m, D_INNER),
        out_shape=jax.ShapeDtypeStruct((m, D_INNER), BF16),
        compiler_params=_cparams("arbitrary"),
        name="sample_gate",
    )(y, act, z, d_e, nw)


ATT_TB = 8


def _sattn_body(q_ref, k_ref, v_ref, o_ref):
    for j in range(ATT_TB):
        prod = k_ref[j] * q_ref[j:j + 1, :]
        for h in range(X_HEADS):
            sl = slice(h * X_HEAD_DIM, (h + 1) * X_HEAD_DIM)
            s = jnp.sum(prod[:, sl], axis=-1, keepdims=True)
            p = jnp.exp(s - jnp.max(s, axis=0, keepdims=True))
            p = p / jnp.sum(p, axis=0, keepdims=True)
            o_ref[j:j + 1, sl] = jnp.sum(p * v_ref[j, :, sl], axis=0, keepdims=True).astype(o_ref.dtype)


def _sample_attn(q, ck, cv):
    m = q.shape[0]
    return pl.pallas_call(
        _sattn_body,
        grid=(m // ATT_TB,),
        in_specs=[pl.BlockSpec((ATT_TB, D_MODEL), lambda i: (i, 0)),
                  pl.BlockSpec((ATT_TB, MEM_LEN, D_MODEL), lambda i: (i, 0, 0)),
                  pl.BlockSpec((ATT_TB, MEM_LEN, D_MODEL), lambda i: (i, 0, 0))],
        out_specs=pl.BlockSpec((ATT_TB, D_MODEL), lambda i: (i, 0)),
        out_shape=jax.ShapeDtypeStruct((m, D_MODEL), F32),
        compiler_params=_cparams("parallel"),
        name="sample_attn",
    )(q, ck, cv)


def _layer_weights(l, norm_mix, w_in, conv_w, conv_b, dt_bias, a_log, d_skip, ssd_norm, w_ssd_proj, pool_w,
                   pool_scale, w_o, norm_x, w_xq, w_xk, w_xv, w_xo, norm_mlp, w_up, w_down):
    wi = w_in[l]
    pad = jnp.zeros((D_MODEL, DT_PAD - SSD_HEADS), F32)
    lane_pad = lambda v, fill: jnp.concatenate([v, jnp.full((LANES - SSD_HEADS,), fill, F32)]).reshape(1, LANES)
    return dict(
        norm_mix=norm_mix[l],
        w_z=wi[:, :OFF_XBC].astype(BF16),
        w_xbc=wi[:, OFF_XBC:OFF_DT].astype(BF16),
        w_pgd=jnp.concatenate([wi[:, OFF_POOL:], wi[:, OFF_DT:OFF_POOL], pad], axis=1).astype(BF16),
        conv_w=conv_w[l], conv_b=conv_b[l].reshape(1, CONV_DIM),
        dt_bias=lane_pad(dt_bias[l], 0.0),
        a_neg=lane_pad(-jnp.exp(a_log[l]), 0.0),
        d_e=jnp.repeat(d_skip[l], SSD_HEAD_DIM).reshape(1, D_INNER),
        ssd_norm=ssd_norm[l].reshape(1, D_INNER),
        w_ssd_proj=w_ssd_proj[l].astype(BF16),
        pool_w=pool_w[l].astype(BF16), pool_scale=pool_scale[l].reshape(1, D_MODEL),
        w_o=w_o[l].astype(BF16), norm_x=norm_x[l], w_xq=w_xq[l].astype(BF16),
        w_xk=w_xk[l].astype(BF16), w_xv=w_xv[l].astype(BF16), w_xo=w_xo[l].astype(BF16),
        norm_mlp=norm_mlp[l], w_up=w_up[l].astype(BF16), w_down=w_down[l].astype(BF16))


def _in_proj(x, p):
    z = _mm(x, p["w_z"], g=p["norm_mix"])
    xbc = _mm(x, p["w_xbc"], g=p["norm_mix"])
    pgd = _mm(x, p["w_pgd"], g=p["norm_mix"], tn=PGD_W)
    return z, xbc, pgd


def _attn_mlp_tail(x, o, p):
    x = _mm(o, p["w_xo"], res=x)
    hmid = _mm(x, p["w_up"], g=p["norm_mlp"], act="relu2", out_dtype=BF16)
    return _mm(hmid, p["w_down"], res=x, tn=512)


def _prompt_layer(x, mem, p, batch, seq):
    mk = _mm(mem, p["w_xk"])
    mv = _mm(mem, p["w_xv"])
    z, xbc, pgd = _in_proj(x, p)
    ya, sst, cst = _ssd_prompt(xbc, z, pgd, p["conv_w"], p["conv_b"], p["dt_bias"], p["a_neg"], p["d_e"],
                               p["ssd_norm"], batch, seq)
    d, pst = _pool_prompt(pgd, batch, seq)
    x = _mix(ya, d, pgd, x, p["w_ssd_proj"], p["pool_w"], p["pool_scale"], p["w_o"])
    q = _mm(x, p["w_xq"], g=p["norm_x"], out_scale=X_HEAD_DIM ** -0.5, out_dtype=BF16)
    o = _attn_prompt(q, mk, mv, batch, seq)
    x = _attn_mlp_tail(x, o, p)
    return x, sst, cst, pst, mk, mv


def _sample_layer(x, st_ssm, st_conv, st_pool, ck, cv, p):
    m = x.shape[0]
    nb = m // SSM_TB
    z, xbc, pgd = _in_proj(x, p)
    ncst, npst, act, xdt, dec, d = _sample_step(xbc, st_conv.reshape(m, -1), p["conv_w"], p["conv_b"], pgd,
                                                p["dt_bias"], p["a_neg"], st_pool.reshape(m, -1), PAST_LEN)
    cols = lambda v: v.reshape(nb, SSM_TB, D_INNER).transpose(0, 2, 1)
    nst, y_c = _sample_ssm(st_ssm.reshape(m, D_INNER, SSD_STATE), cols(xdt), cols(dec), act)
    y = y_c.transpose(0, 2, 1).reshape(m, D_INNER)
    ya = _sample_gate(y, act, z, p["d_e"], p["ssd_norm"])
    x = _mix(ya, d, pgd, x, p["w_ssd_proj"], p["pool_w"], p["pool_scale"], p["w_o"])
    q = _mm(x, p["w_xq"], g=p["norm_x"], out_scale=X_HEAD_DIM ** -0.5)
    o = _sample_attn(q, ck.reshape(m, MEM_LEN, D_MODEL), cv.reshape(m, MEM_LEN, D_MODEL))
    x = _attn_mlp_tail(x, o, p)
    return x, nst, ncst, npst


def kernel(x_prompt, x_sample, mem_prompt, state_ssm, state_conv, state_pool, cache_mem_k, cache_mem_v,
           norm_mix, w_in, conv_w, conv_b, dt_bias, a_log, d_skip, ssd_norm, w_ssd_proj, pool_w, pool_scale,
           w_o, norm_x, w_xq, w_xk, w_xv, w_xo, norm_mlp, w_up, w_down, norm_final):
    bp, seq, _ = x_prompt.shape
    bs = x_sample.shape[0]
    depth = w_in.shape[0]
    assert x_sample.shape[1] == 1 and seq % CHUNK == 0
    hp = x_prompt.reshape(bp * seq, D_MODEL)
    hs = x_sample.reshape(bs, D_MODEL)
    mem = mem_prompt.reshape(bp * MEM_LEN, D_MODEL)
    outs = [[] for _ in range(8)]
    for l in range(depth):
        p = _layer_weights(l, norm_mix, w_in, conv_w, conv_b, dt_bias, a_log, d_skip, ssd_norm, w_ssd_proj,
                           pool_w, pool_scale, w_o, norm_x, w_xq, w_xk, w_xv, w_xo, norm_mlp, w_up, w_down)
        hp, sst, cst, pst, mk, mv = _prompt_layer(hp, mem, p, bp, seq)
        hs, nst, ncst, npst = _sample_layer(hs, state_ssm[l], state_conv[l], state_pool[l],
                                            cache_mem_k[l], cache_mem_v[l], p)
        outs[0].append(sst.reshape(bp, SSD_HEADS, SSD_HEAD_DIM, SSD_STATE))
        outs[1].append(cst)
        outs[2].append(pst)
        outs[3].append(mk.reshape(bp, MEM_LEN, X_HEADS, X_HEAD_DIM))
        outs[4].append(mv.reshape(bp, MEM_LEN, X_HEADS, X_HEAD_DIM))
        outs[5].append(nst.reshape(bs, SSD_HEADS, SSD_HEAD_DIM, SSD_STATE))
        outs[6].append(ncst.reshape(bs, CONV_WIDTH - 1, CONV_DIM))
        outs[7].append(npst.reshape(bs, POOL_BUF, D_MODEL))
    y_prompt = _rmsnorm(hp, norm_final).reshape(bp, seq, D_MODEL)
    y_sample = _rmsnorm(hs, norm_final).reshape(bs, 1, D_MODEL)
    return (y_prompt, y_sample) + tuple(jnp.stack(o) for o in outs)
```

```python
import functools

import jax
import jax.numpy as jnp
from jax import lax
from jax.experimental import pallas as pl
from jax.experimental.pallas import tpu as pltpu

F32 = jnp.float32
BF16 = jnp.bfloat16

D_MODEL = 1024
D_INNER = 2048
SSD_HEAD_DIM = 64
SSD_HEADS = 32
SSD_GROUPS = 4
SSD_STATE = 128
GROUP_W = D_INNER // SSD_GROUPS
CONV_WIDTH = 4
CONV_DIM = D_INNER + 2 * SSD_GROUPS * SSD_STATE
CHUNK = 128
POOL_WINDOWS = (2, 4, 8, 16)
POOL_GW = 256
POOL_BUF = 15
POOL_HIST = 16
MEM_LEN = 256
X_HEADS = 4
X_HEAD_DIM = 256
D_FF = 4096
EPS = 1e-6
PAST_LEN = 16384
OFF_XBC = D_INNER
OFF_DT = OFF_XBC + CONV_DIM
OFF_POOL = OFF_DT + SSD_HEADS
LANES = 128
SUBLANES = 8
XZD_Z = CONV_DIM
XZD_DT = CONV_DIM + D_INNER
XZD_W = XZD_DT + LANES
PG_W = 3 * D_MODEL
VMEM_LIMIT = 52 * 1024 * 1024


def _cparams(*sem):
    return pltpu.CompilerParams(dimension_semantics=sem, vmem_limit_bytes=VMEM_LIMIT)


def _resident(shape):
    return pl.BlockSpec(shape, lambda *_: (0,) * len(shape), pipeline_mode=pl.Buffered(1))


def _dot(a, b):
    return jnp.dot(a, b, preferred_element_type=F32)


def _dot_nt(a, b):
    return lax.dot_general(a, b, (((1,), (1,)), ((), ())), preferred_element_type=F32)


def _split3(x):
    hi = x.astype(BF16)
    r1 = x - hi.astype(F32)
    mid = r1.astype(BF16)
    lo = (r1 - mid.astype(F32)).astype(BF16)
    return hi, mid, lo


def _silu(x):
    return x * jax.nn.sigmoid(x)


def _softplus(x):
    return jnp.maximum(x, 0.0) + jnp.log1p(jnp.exp(-jnp.abs(x)))


def _rms(x, g):
    return x * lax.rsqrt(jnp.mean(x * x, axis=-1, keepdims=True) + EPS) * g


def _group_rms(y, nw):
    parts = []
    for g in range(SSD_GROUPS):
        yg = y[:, g * GROUP_W:(g + 1) * GROUP_W]
        parts.append(yg * lax.rsqrt(jnp.mean(yg * yg, axis=-1, keepdims=True) + EPS))
    return jnp.concatenate(parts, axis=1) * nw


def _pool_delta(cur_of, hist_of, cnt_of):
    outs = []
    for gi, w in enumerate(POOL_WINDOWS):
        lo = gi * POOL_GW
        cur = cur_of(lo)
        acc = cur
        for k in range(1, w):
            acc = acc + hist_of(k, lo)
        outs.append(acc / cnt_of(w) - cur)
    return jnp.concatenate(outs, axis=1)


def _mm_body(*refs, norm, has_res, out_scale):
    it = iter(refs)
    x_ref = next(it)
    w_ref = next(it)
    g_ref = next(it) if norm else None
    r_ref = next(it) if has_res else None
    o_ref = next(it)
    u_ref = next(it)

    @pl.when(pl.program_id(1) == 0)
    def _():
        x = x_ref[...].astype(F32)
        if norm:
            x = _rms(x, g_ref[...])
        u_ref[...] = x.astype(BF16)

    acc = _dot(u_ref[...], w_ref[...])
    if out_scale != 1.0:
        acc = acc * out_scale
    if has_res:
        acc = acc + r_ref[...]
    o_ref[...] = acc.astype(o_ref.dtype)


def _mm(x, w, *, g=None, res=None, out_scale=1.0, out_dtype=F32, tm=512, tn=1024):
    m, k = x.shape
    n = w.shape[1]
    tm = min(tm, m)
    tn = min(tn, n)
    assert m % tm == 0 and n % tn == 0, (m, n, tm, tn)
    in_specs = [pl.BlockSpec((tm, k), lambda i, j: (i, 0)),
                pl.BlockSpec((k, tn), lambda i, j: (0, j))]
    args = [x, w]
    if g is not None:
        in_specs.append(pl.BlockSpec((1, k), lambda i, j: (0, 0)))
        args.append(g.reshape(1, k))
    if res is not None:
        in_specs.append(pl.BlockSpec((tm, tn), lambda i, j: (i, j)))
        args.append(res)
    return pl.pallas_call(
        functools.partial(_mm_body, norm=g is not None, has_res=res is not None, out_scale=out_scale),
        grid=(m // tm, n // tn),
        in_specs=in_specs,
        out_specs=pl.BlockSpec((tm, tn), lambda i, j: (i, j)),
        out_shape=jax.ShapeDtypeStruct((m, n), out_dtype),
        scratch_shapes=[pltpu.VMEM((tm, k), BF16)],
        compiler_params=_cparams("parallel", "arbitrary"),
        name="mm",
    )(*args)


def _kv_body(x_ref, wk_ref, wv_ref, k_ref, v_ref, u_ref):
    @pl.when(pl.program_id(1) == 0)
    def _():
        u_ref[...] = x_ref[...].astype(BF16)

    k_ref[...] = _dot(u_ref[...], wk_ref[...])
    v_ref[...] = _dot(u_ref[...], wv_ref[...])


def _kv_proj(mem, wk, wv, tm=512):
    m, k = mem.shape
    depth = wk.shape[0]
    tm = min(tm, m)
    wspec = pl.BlockSpec((None, k, D_MODEL), lambda i, l: (l, 0, 0))
    ospec = pl.BlockSpec((None, tm, D_MODEL), lambda i, l: (l, i, 0))
    oshape = jax.ShapeDtypeStruct((depth, m, D_MODEL), F32)
    return pl.pallas_call(
        _kv_body,
        grid=(m // tm, depth),
        in_specs=[pl.BlockSpec((tm, k), lambda i, l: (i, 0)), wspec, wspec],
        out_specs=[ospec, ospec],
        out_shape=[oshape, oshape],
        scratch_shapes=[pltpu.VMEM((tm, k), BF16)],
        compiler_params=_cparams("parallel", "arbitrary"),
        name="kv_proj",
    )(mem, wk, wv)


def _rmsnorm_body(x_ref, g_ref, o_ref):
    o_ref[...] = _rms(x_ref[...], g_ref[...])


def _rmsnorm(x, g, tm=512):
    m, k = x.shape
    tm = min(tm, m)
    return pl.pallas_call(
        _rmsnorm_body,
        grid=(m // tm,),
        in_specs=[pl.BlockSpec((tm, k), lambda i: (i, 0)), pl.BlockSpec((1, k), lambda i: (0, 0))],
        out_specs=pl.BlockSpec((tm, k), lambda i: (i, 0)),
        out_shape=jax.ShapeDtypeStruct((m, k), F32),
        compiler_params=_cparams("parallel"),
        name="rmsnorm",
    )(x, g.reshape(1, k))


def _ssd_body(x_ref, g_ref, w_ref, cw_ref, cb_ref, dtb_ref, a_ref, de_ref, nw_ref,
              ya_ref, sst_ref, cst_ref, cbuf, st_scr, *, nchunks):
    c = pl.program_id(1)
    L = CHUNK

    @pl.when(c == 0)
    def _():
        cbuf[0:SUBLANES, :] = jnp.zeros((SUBLANES, CONV_DIM), F32)
        st_scr[...] = jnp.zeros_like(st_scr)

    u = _rms(x_ref[...], g_ref[...]).astype(BF16)
    xbc = _dot(u, w_ref[:, 0:XZD_Z])
    z = _dot(u, w_ref[:, XZD_Z:XZD_DT])
    dtr = _dot(u, w_ref[:, XZD_DT:XZD_W])

    cbuf[SUBLANES:SUBLANES + L, :] = xbc
    conv = cb_ref[...] + cbuf[5:5 + L, :] * cw_ref[0:1, :]
    conv = conv + cbuf[6:6 + L, :] * cw_ref[1:2, :]
    conv = conv + cbuf[7:7 + L, :] * cw_ref[2:3, :]
    conv = conv + xbc * cw_ref[3:4, :]
    cbuf[0:SUBLANES, :] = xbc[L - SUBLANES:L, :]
    act = _silu(conv)
    xs = act[:, :D_INNER]

    dt = _softplus(dtr + dtb_ref[...])
    a = dt * a_ref[...]
    row = lax.broadcasted_iota(jnp.int32, (L, L), 0)
    col = lax.broadcasted_iota(jnp.int32, (L, L), 1)
    tri = row >= col
    ltri = tri.astype(BF16)
    a_hi, a_mid, a_lo = _split3(a)
    cs = _dot(ltri, a_hi) + _dot(ltri, a_mid) + _dot(ltri, a_lo)
    cs_t = cs.T
    dt_t = dt.T
    cs_last = cs[L - 1:L, :]
    wdec = dt * jnp.exp(cs_last - cs)
    lt64 = col < SSD_HEAD_DIM
    lt64_row = lax.broadcasted_iota(jnp.int32, (1, L), 1) < SSD_HEAD_DIM

    y_pairs = []
    for g in range(SSD_GROUPS):
        bm = act[:, D_INNER + g * SSD_STATE:D_INNER + (g + 1) * SSD_STATE]
        cm = act[:, D_INNER + (SSD_GROUPS + g) * SSD_STATE:D_INNER + (SSD_GROUPS + g + 1) * SSD_STATE]
        cb = _dot_nt(cm.astype(BF16), bm.astype(BF16))
        xw_pairs = []
        dec_pairs = []
        for kk in range(4):
            k = g * 4 + kk
            lhs_m, lhs_c, colbs = [], [], []
            for h in (2 * k, 2 * k + 1):
                colb = jnp.broadcast_to(cs[:, h:h + 1], (L, L))
                seg = colb - cs_t[h:h + 1, :]
                lm = jnp.where(tri, jnp.exp(seg), 0.0)
                lhs_m.append(cb * lm * dt_t[h:h + 1, :])
                lhs_c.append(cm * jnp.exp(colb))
                colbs.append(colb)
            lhs = jnp.concatenate(lhs_m + lhs_c, axis=1).astype(BF16)
            xs_p = xs[:, k * LANES:(k + 1) * LANES]
            st_p = st_scr[:, k * LANES:(k + 1) * LANES]
            rhs = jnp.concatenate([jnp.where(lt64, xs_p, 0.0), jnp.where(lt64, 0.0, xs_p),
                                   jnp.where(lt64, st_p, 0.0), jnp.where(lt64, 0.0, st_p)],
                                  axis=0).astype(BF16)
            y_pairs.append(_dot(lhs, rhs))
            wb = jnp.where(lt64, jnp.broadcast_to(wdec[:, 2 * k:2 * k + 1], (L, L)),
                           jnp.broadcast_to(wdec[:, 2 * k + 1:2 * k + 2], (L, L)))
            xw_pairs.append(xs_p * wb)
            dec_pairs.append(jnp.exp(jnp.where(lt64_row, colbs[0][L - 1:L, :], colbs[1][L - 1:L, :])))
        xw = jnp.concatenate(xw_pairs, axis=1).astype(BF16)
        dec = jnp.concatenate(dec_pairs, axis=1)
        inc = _dot(bm.T.astype(BF16), xw)
        sl = slice(g * GROUP_W, (g + 1) * GROUP_W)
        st_scr[:, sl] = st_scr[:, sl] * dec + inc

    y = jnp.concatenate(y_pairs, axis=1)
    y = y + de_ref[...] * xs
    y = y * _silu(z)
    ya_ref[...] = _group_rms(y, nw_ref[...]).astype(ya_ref.dtype)

    @pl.when(c == nchunks - 1)
    def _():
        for k in range(D_INNER // LANES):
            sst_ref[0, k * LANES:(k + 1) * LANES, :] = st_scr[:, k * LANES:(k + 1) * LANES].T
        cst_ref[0] = cbuf[SUBLANES - (CONV_WIDTH - 1):SUBLANES, :]


def _ssd_prompt(x, p, batch, seq):
    nchunks = seq // CHUNK
    tok = lambda b, c: (b * nchunks + c, 0)
    const = lambda b, c: (0, 0)
    return pl.pallas_call(
        functools.partial(_ssd_body, nchunks=nchunks),
        grid=(batch, nchunks),
        in_specs=[pl.BlockSpec((CHUNK, D_MODEL), tok),
                  pl.BlockSpec((1, D_MODEL), const),
                  _resident((D_MODEL, XZD_W)),
                  pl.BlockSpec((CONV_WIDTH, CONV_DIM), const),
                  pl.BlockSpec((1, CONV_DIM), const),
                  pl.BlockSpec((1, LANES), const),
                  pl.BlockSpec((1, LANES), const),
                  pl.BlockSpec((1, D_INNER), const),
                  pl.BlockSpec((1, D_INNER), const)],
        out_specs=[pl.BlockSpec((CHUNK, D_INNER), tok),
                   pl.BlockSpec((1, D_INNER, SSD_STATE), lambda b, c: (b, 0, 0)),
                   pl.BlockSpec((1, CONV_WIDTH - 1, CONV_DIM), lambda b, c: (b, 0, 0))],
        out_shape=[jax.ShapeDtypeStruct((batch * seq, D_INNER), BF16),
                   jax.ShapeDtypeStruct((batch, D_INNER, SSD_STATE), F32),
                   jax.ShapeDtypeStruct((batch, CONV_WIDTH - 1, CONV_DIM), F32)],
        scratch_shapes=[pltpu.VMEM((SUBLANES + CHUNK, CONV_DIM), F32),
                        pltpu.VMEM((SSD_STATE, D_INNER), F32)],
        compiler_params=_cparams("parallel", "arbitrary"),
        name="ssd_prompt",
    )(x, p["norm_mix"], p["w_xzd"], p["conv_w"], p["conv_b"], p["dt_bias"], p["a_neg"], p["d_e"], p["ssd_norm"])


def _merge(x, ya, d, ga, gb, wsp_ref, pw_ref, ps_ref, wo_ref):
    ssd = _dot(ya, wsp_ref[...])
    pool = jnp.concatenate(
        [_dot(d[:, g * POOL_GW:(g + 1) * POOL_GW], pw_ref[g]) for g in range(len(POOL_WINDOWS))], axis=1)
    pool = pool * ps_ref[...]
    merged = jax.nn.sigmoid(ga) * ssd + jax.nn.sigmoid(gb) * pool
    return x + _dot(merged.astype(BF16), wo_ref[...])


def _mixp_body(x_ref, ya_ref, g_ref, wpg_ref, wsp_ref, pw_ref, ps_ref, wo_ref, o_ref, pst_ref, pbuf,
               *, tt, ntiles):
    c = pl.program_id(1)

    @pl.when(c == 0)
    def _():
        pbuf[0:POOL_HIST, :] = jnp.zeros((POOL_HIST, D_MODEL), F32)

    x = x_ref[...]
    u = _rms(x, g_ref[...]).astype(BF16)
    pu = _dot(u, wpg_ref[:, 0:D_MODEL])
    ga = _dot(u, wpg_ref[:, D_MODEL:2 * D_MODEL])
    gb = _dot(u, wpg_ref[:, 2 * D_MODEL:3 * D_MODEL])
    pbuf[POOL_HIST:POOL_HIST + tt, :] = pu
    pos = c * tt + lax.broadcasted_iota(jnp.int32, (tt, 1), 0)
    d = _pool_delta(lambda lo: pu[:, lo:lo + POOL_GW],
                    lambda k, lo: pbuf[POOL_HIST - k:POOL_HIST - k + tt, lo:lo + POOL_GW],
                    lambda w: jnp.minimum(w, pos + 1).astype(F32))
    o_ref[...] = _merge(x, ya_ref[...], d.astype(BF16), ga, gb, wsp_ref, pw_ref, ps_ref, wo_ref)

    @pl.when(c == ntiles - 1)
    def _():
        pst_ref[0] = pbuf[tt + 1:tt + POOL_HIST, :]

    pbuf[0:POOL_HIST, :] = pbuf[tt:tt + POOL_HIST, :]


def _mix_prompt(x, ya, p, batch, seq, tt=256):
    tt = min(tt, seq)
    assert seq % tt == 0
    ntiles = seq // tt
    tok = lambda b, c: (b * ntiles + c, 0)
    return pl.pallas_call(
        functools.partial(_mixp_body, tt=tt, ntiles=ntiles),
        grid=(batch, ntiles),
        in_specs=[pl.BlockSpec((tt, D_MODEL), tok),
                  pl.BlockSpec((tt, D_INNER), tok),
                  pl.BlockSpec((1, D_MODEL), lambda b, c: (0, 0)),
                  _resident((D_MODEL, PG_W)),
                  _resident((D_INNER, D_MODEL)),
                  _resident((len(POOL_WINDOWS), POOL_GW, POOL_GW)),
                  pl.BlockSpec((1, D_MODEL), lambda b, c: (0, 0)),
                  _resident((D_MODEL, D_MODEL))],
        out_specs=[pl.BlockSpec((tt, D_MODEL), tok),
                   pl.BlockSpec((1, POOL_BUF, D_MODEL), lambda b, c: (b, 0, 0))],
        out_shape=[jax.ShapeDtypeStruct((batch * seq, D_MODEL), F32),
                   jax.ShapeDtypeStruct((batch, POOL_BUF, D_MODEL), F32)],
        scratch_shapes=[pltpu.VMEM((POOL_HIST + tt, D_MODEL), F32)],
        compiler_params=_cparams("parallel", "arbitrary"),
        name="mix_prompt",
    )(x, ya, p["norm_mix"], p["w_pg"], p["w_ssd_proj"], p["pool_w"], p["pool_scale"], p["w_o"])


def _attn_body(x_ref, g_ref, wq_ref, k_ref, v_ref, wo_ref, o_ref):
    x = x_ref[...]
    u = _rms(x, g_ref[...]).astype(BF16)
    q = (_dot(u, wq_ref[...]) * (X_HEAD_DIM ** -0.5)).astype(BF16)
    outs = []
    for h in range(X_HEADS):
        sl = slice(h * X_HEAD_DIM, (h + 1) * X_HEAD_DIM)
        s = _dot_nt(q[:, sl], k_ref[:, sl].astype(BF16))
        p = jnp.exp(s - jnp.max(s, axis=-1, keepdims=True))
        p = p / jnp.sum(p, axis=-1, keepdims=True)
        outs.append(_dot(p.astype(BF16), v_ref[:, sl].astype(BF16)))
    o = jnp.concatenate(outs, axis=1).astype(BF16)
    o_ref[...] = x + _dot(o, wo_ref[...])


def _attn_prompt(x, mk, mv, l, p, batch, seq, tm=512):
    tm = min(tm, seq)
    assert seq % tm == 0
    nt = seq // tm
    tok = lambda b, i: (b * nt + i, 0)
    kv = pl.BlockSpec((None, MEM_LEN, D_MODEL), lambda b, i: (l, b, 0))
    return pl.pallas_call(
        _attn_body,
        grid=(batch, nt),
        in_specs=[pl.BlockSpec((tm, D_MODEL), tok),
                  pl.BlockSpec((1, D_MODEL), lambda b, i: (0, 0)),
                  _resident((D_MODEL, D_MODEL)), kv, kv,
                  _resident((D_MODEL, D_MODEL))],
        out_specs=pl.BlockSpec((tm, D_MODEL), tok),
        out_shape=jax.ShapeDtypeStruct((batch * seq, D_MODEL), F32),
        compiler_params=_cparams("parallel", "parallel"),
        name="attn_prompt",
    )(x, p["norm_x"], p["w_xq"], mk, mv, p["w_xo"])


def _mlp_body(x_ref, g_ref, wu_ref, wd_ref, o_ref, xm_ref):
    @pl.when(pl.program_id(1) == 0)
    def _():
        x = x_ref[...]
        xm_ref[...] = _rms(x, g_ref[...]).astype(BF16)
        o_ref[...] = x

    h = jnp.square(jnp.maximum(_dot(xm_ref[...], wu_ref[...]), 0.0)).astype(BF16)
    o_ref[...] += _dot(h, wd_ref[...])


def _mlp(x, p, tm=1024, tf=1024):
    m = x.shape[0]
    tm = min(tm, m)
    assert m % tm == 0
    return pl.pallas_call(
        _mlp_body,
        grid=(m // tm, D_FF // tf),
        in_specs=[pl.BlockSpec((tm, D_MODEL), lambda i, k: (i, 0)),
                  pl.BlockSpec((1, D_MODEL), lambda i, k: (0, 0)),
                  pl.BlockSpec((D_MODEL, tf), lambda i, k: (0, k)),
                  pl.BlockSpec((tf, D_MODEL), lambda i, k: (k, 0))],
        out_specs=pl.BlockSpec((tm, D_MODEL), lambda i, k: (i, 0)),
        out_shape=jax.ShapeDtypeStruct((m, D_MODEL), F32),
        scratch_shapes=[pltpu.VMEM((tm, D_MODEL), BF16)],
        compiler_params=_cparams("parallel", "arbitrary"),
        name="mlp",
    )(x, p["norm_mlp"], p["w_up"], p["w_down"])


STEP_TB = 32


def _step_body(x_ref, g_ref, w_ref, wpg_ref, cst_ref, cw_ref, cb_ref, dtb_ref, a_ref, pst_ref,
               ncst_ref, npst_ref, act_ref, xdt_ref, dec_ref, d_ref, z_ref, pg_ref, *, start_pos):
    u = _rms(x_ref[...], g_ref[...]).astype(BF16)
    xbc = _dot(u, w_ref[:, 0:XZD_Z])
    z_ref[...] = _dot(u, w_ref[:, XZD_Z:XZD_DT])
    dtr = _dot(u, w_ref[:, XZD_DT:XZD_W])
    pg = _dot(u, wpg_ref[...])
    pg_ref[...] = pg
    pu = pg[:, 0:D_MODEL]

    conv = cb_ref[...]
    for k in range(CONV_WIDTH - 1):
        conv = conv + cst_ref[:, k * CONV_DIM:(k + 1) * CONV_DIM] * cw_ref[k:k + 1, :]
    conv = conv + xbc * cw_ref[CONV_WIDTH - 1:CONV_WIDTH, :]
    for k in range(CONV_WIDTH - 2):
        ncst_ref[:, k * CONV_DIM:(k + 1) * CONV_DIM] = cst_ref[:, (k + 1) * CONV_DIM:(k + 2) * CONV_DIM]
    ncst_ref[:, (CONV_WIDTH - 2) * CONV_DIM:] = xbc
    act = _silu(conv)
    act_ref[...] = act

    dt = _softplus(dtr + dtb_ref[...])
    dec = jnp.exp(dt * a_ref[...])
    lt64 = lax.broadcasted_iota(jnp.int32, (1, LANES), 1) < SSD_HEAD_DIM
    m = dt.shape[0]
    for k in range(D_INNER // LANES):
        dt_e = jnp.where(lt64, jnp.broadcast_to(dt[:, 2 * k:2 * k + 1], (m, LANES)),
                         jnp.broadcast_to(dt[:, 2 * k + 1:2 * k + 2], (m, LANES)))
        dec_e = jnp.where(lt64, jnp.broadcast_to(dec[:, 2 * k:2 * k + 1], (m, LANES)),
                          jnp.broadcast_to(dec[:, 2 * k + 1:2 * k + 2], (m, LANES)))
        xdt_ref[:, k * LANES:(k + 1) * LANES] = act[:, k * LANES:(k + 1) * LANES] * dt_e
        dec_ref[:, k * LANES:(k + 1) * LANES] = dec_e

    d = _pool_delta(lambda lo: pu[:, lo:lo + POOL_GW],
                    lambda k, lo: pst_ref[:, (POOL_BUF - k) * D_MODEL + lo:(POOL_BUF - k) * D_MODEL + lo + POOL_GW],
                    lambda w: float(min(w, start_pos + 1)))
    d_ref[...] = d.astype(d_ref.dtype)
    npst_ref[:, :(POOL_BUF - 1) * D_MODEL] = pst_ref[:, D_MODEL:]
    npst_ref[:, (POOL_BUF - 1) * D_MODEL:] = pu


def _sample_step(x, cst, pst, p, start_pos):
    m = x.shape[0]
    tm = min(STEP_TB, m)
    const = lambda *shape: pl.BlockSpec(shape, lambda i: (0,) * len(shape))
    rows = lambda width: pl.BlockSpec((tm, width), lambda i: (i, 0))
    widths = [(CONV_WIDTH - 1) * CONV_DIM, POOL_BUF * D_MODEL, CONV_DIM, D_INNER, D_INNER, D_MODEL, D_INNER, PG_W]
    dtypes = [F32, F32, F32, F32, F32, BF16, F32, F32]
    return pl.pallas_call(
        functools.partial(_step_body, start_pos=start_pos),
        grid=(m // tm,),
        in_specs=[rows(D_MODEL), const(1, D_MODEL), _resident((D_MODEL, XZD_W)), _resident((D_MODEL, PG_W)),
                  rows((CONV_WIDTH - 1) * CONV_DIM), const(CONV_WIDTH, CONV_DIM), const(1, CONV_DIM),
                  const(1, LANES), const(1, LANES), rows(POOL_BUF * D_MODEL)],
        out_specs=[rows(w) for w in widths],
        out_shape=[jax.ShapeDtypeStruct((m, w), dt) for w, dt in zip(widths, dtypes)],
        compiler_params=_cparams("parallel"),
        name="sample_step",
    )(x, p["norm_mix"], p["w_xzd"], p["w_pg"], cst, p["conv_w"], p["conv_b"], p["dt_bias"], p["a_neg"], pst)


SSM_TB = 8


def _ssm_body(s_ref, xdt_ref, dec_ref, act_ref, *rest):
    so_ref, y_ref = rest[-2:]
    for j in range(SSM_TB):
        for g in range(SSD_GROUPS):
            rows = slice(g * GROUP_W, (g + 1) * GROUP_W)
            b_row = act_ref[j:j + 1, D_INNER + g * SSD_STATE:D_INNER + (g + 1) * SSD_STATE]
            c_row = act_ref[j:j + 1, D_INNER + (SSD_GROUPS + g) * SSD_STATE:
                            D_INNER + (SSD_GROUPS + g + 1) * SSD_STATE]
            sn = s_ref[j, rows, :] * dec_ref[0, rows, j:j + 1] + xdt_ref[0, rows, j:j + 1] * b_row
            so_ref[j, rows, :] = sn
            y_ref[0, rows, j:j + 1] = jnp.sum(sn * c_row, axis=-1, keepdims=True)


def _sample_ssm(states, l, prev, xdt_c, dec_c, act):
    depth, m = states.shape[:2]
    nb = m // SSM_TB
    st_spec = pl.BlockSpec((None, SSM_TB, D_INNER, SSD_STATE), lambda i: (l, i, 0, 0))
    col_spec = pl.BlockSpec((1, D_INNER, SSM_TB), lambda i: (i, 0, 0))
    in_specs = [st_spec, col_spec, col_spec, pl.BlockSpec((SSM_TB, CONV_DIM), lambda i: (i, 0))]
    args = [states, xdt_c, dec_c, act]
    aliases = {}
    if prev is not None:
        in_specs.append(pl.BlockSpec(memory_space=pl.ANY))
        args.append(prev)
        aliases = {len(args) - 1: 0}
    return pl.pallas_call(
        _ssm_body,
        grid=(nb,),
        in_specs=in_specs,
        out_specs=[st_spec, col_spec],
        out_shape=[jax.ShapeDtypeStruct(states.shape, F32),
                   jax.ShapeDtypeStruct((nb, D_INNER, SSM_TB), F32)],
        input_output_aliases=aliases,
        compiler_params=_cparams("parallel"),
        name="sample_ssm",
    )(*args)


def _gate_body(y_ref, act_ref, z_ref, de_ref, nw_ref, o_ref):
    y = y_ref[...] + de_ref[...] * act_ref[:, :D_INNER]
    y = y * _silu(z_ref[...])
    o_ref[...] = _group_rms(y, nw_ref[...]).astype(o_ref.dtype)


def _sample_gate(y, act, z, d_e, nw):
    m = y.shape[0]
    full = lambda *shape: pl.BlockSpec(shape, lambda i: (0,) * len(shape))
    return pl.pallas_call(
        _gate_body,
        grid=(1,),
        in_specs=[full(m, D_INNER), full(m, CONV_DIM), full(m, D_INNER), full(1, D_INNER), full(1, D_INNER)],
        out_specs=full(m, D_INNER),
        out_shape=jax.ShapeDtypeStruct((m, D_INNER), BF16),
        compiler_params=_cparams("arbitrary"),
        name="sample_gate",
    )(y, act, z, d_e, nw)


def _mixs_body(ya_ref, d_ref, ga_ref, gb_ref, x_ref, wsp_ref, pw_ref, ps_ref, wo_ref, o_ref):
    o_ref[...] = _merge(x_ref[...], ya_ref[...], d_ref[...], ga_ref[...], gb_ref[...],
                        wsp_ref, pw_ref, ps_ref, wo_ref)


def _mix_sample(ya, d, pg, x, p):
    m = x.shape[0]
    blk = lambda width, j: pl.BlockSpec((m, width), lambda i: (0, j))
    const = lambda *shape: pl.BlockSpec(shape, lambda i: (0,) * len(shape))
    return pl.pallas_call(
        _mixs_body,
        grid=(1,),
        in_specs=[blk(D_INNER, 0), blk(D_MODEL, 0), blk(D_MODEL, 1), blk(D_MODEL, 2), blk(D_MODEL, 0),
                  const(D_INNER, D_MODEL), const(len(POOL_WINDOWS), POOL_GW, POOL_GW), const(1, D_MODEL),
                  const(D_MODEL, D_MODEL)],
        out_specs=blk(D_MODEL, 0),
        out_shape=jax.ShapeDtypeStruct((m, D_MODEL), F32),
        compiler_params=_cparams("arbitrary"),
        name="mix_sample",
    )(ya, d, pg, pg, x, p["w_ssd_proj"], p["pool_w"], p["pool_scale"], p["w_o"])


ATT_TB = 8
KV_ROWS = MEM_LEN * SUBLANES
HALVES = X_HEAD_DIM // LANES
assert HALVES * X_HEADS == SUBLANES


def _sattn_body(q_ref, k_ref, v_ref, o_ref):
    for j in range(ATT_TB):
        k3 = k_ref[j].reshape(MEM_LEN, SUBLANES, LANES)
        v3 = v_ref[j].reshape(MEM_LEN, SUBLANES, LANES)
        prod = k3 * q_ref[j][None]
        prod = prod + pltpu.roll(prod, X_HEADS, axis=1)
        s = jnp.sum(prod, axis=-1, keepdims=True)
        e = jnp.exp(s - jnp.max(s, axis=0, keepdims=True))
        pr = e / jnp.sum(e, axis=0, keepdims=True)
        o_ref[j] = jnp.sum(pr * v3, axis=0)


def _kv_rows(c):
    depth, m = c.shape[:2]
    c = c.reshape(depth, m, MEM_LEN, X_HEADS, HALVES, LANES).transpose(0, 1, 2, 4, 3, 5)
    return c.reshape(depth, m, KV_ROWS, LANES)


def _sample_attn(q, ck, cv, l):
    m = q.shape[0]
    q8 = q.reshape(m, X_HEADS, HALVES, LANES).transpose(0, 2, 1, 3).reshape(m, SUBLANES, LANES)
    kv = pl.BlockSpec((None, ATT_TB, KV_ROWS, LANES), lambda i: (l, i, 0, 0))
    qo = pl.BlockSpec((ATT_TB, SUBLANES, LANES), lambda i: (i, 0, 0))
    o8 = pl.pallas_call(
        _sattn_body,
        grid=(m // ATT_TB,),
        in_specs=[qo, kv, kv],
        out_specs=qo,
        out_shape=jax.ShapeDtypeStruct((m, SUBLANES, LANES), F32),
        compiler_params=_cparams("parallel"),
        name="sample_attn",
    )(q8, ck, cv)
    return o8.reshape(m, HALVES, X_HEADS, LANES).transpose(0, 2, 1, 3).reshape(m, D_MODEL)


def _layer_weights(l, norm_mix, w_in, conv_w, conv_b, dt_bias, a_log, d_skip, ssd_norm, w_ssd_proj, pool_w,
                   pool_scale, w_o, norm_x, w_xq, w_xo, norm_mlp, w_up, w_down):
    wi = w_in[l]
    pad = jnp.zeros((D_MODEL, LANES - SSD_HEADS), F32)
    lane_pad = lambda v: jnp.concatenate([v, jnp.zeros((LANES - SSD_HEADS,), F32)]).reshape(1, LANES)
    row = lambda v: v.reshape(1, -1)
    return dict(
        norm_mix=row(norm_mix[l]),
        w_xzd=jnp.concatenate([wi[:, OFF_XBC:OFF_DT], wi[:, :OFF_XBC], wi[:, OFF_DT:OFF_POOL], pad],
                              axis=1).astype(BF16),
        w_pg=wi[:, OFF_POOL:].astype(BF16),
        conv_w=conv_w[l], conv_b=row(conv_b[l]),
        dt_bias=lane_pad(dt_bias[l]),
        a_neg=lane_pad(-jnp.exp(a_log[l])),
        d_e=row(jnp.repeat(d_skip[l], SSD_HEAD_DIM)),
        ssd_norm=row(ssd_norm[l]),
        w_ssd_proj=w_ssd_proj[l].astype(BF16),
        pool_w=pool_w[l].astype(BF16), pool_scale=row(pool_scale[l]),
        w_o=w_o[l].astype(BF16), norm_x=row(norm_x[l]), w_xq=w_xq[l].astype(BF16),
        w_xo=w_xo[l].astype(BF16),
        norm_mlp=row(norm_mlp[l]), w_up=w_up[l].astype(BF16), w_down=w_down[l].astype(BF16))


def _prompt_layer(x, mk, mv, l, p, batch, seq):
    ya, sst, cst = _ssd_prompt(x, p, batch, seq)
    x, pst = _mix_prompt(x, ya, p, batch, seq)
    x = _attn_prompt(x, mk, mv, l, p, batch, seq)
    x = _mlp(x, p)
    return x, sst, cst, pst


def _sample_layer(x, states, new_states, st_conv, st_pool, ck, cv, l, p):
    m = x.shape[0]
    nb = m // SSM_TB
    ncst, npst, act, xdt, dec, d, z, pg = _sample_step(x, st_conv.reshape(m, -1), st_pool.reshape(m, -1), p,
                                                       PAST_LEN)
    cols = lambda v: v.reshape(nb, SSM_TB, D_INNER).transpose(0, 2, 1)
    new_states, y_c = _sample_ssm(states, l, new_states, cols(xdt), cols(dec), act)
    y = y_c.transpose(0, 2, 1).reshape(m, D_INNER)
    ya = _sample_gate(y, act, z, p["d_e"], p["ssd_norm"])
    x = _mix_sample(ya, d, pg, x, p)
    q = _mm(x, p["w_xq"], g=p["norm_x"], out_scale=X_HEAD_DIM ** -0.5)
    o = _sample_attn(q, ck, cv, l)
    x = _mm(o, p["w_xo"], res=x)
    x = _mlp(x, p)
    return x, new_states, ncst, npst


def kernel(x_prompt, x_sample, mem_prompt, state_ssm, state_conv, state_pool, cache_mem_k, cache_mem_v,
           norm_mix, w_in, conv_w, conv_b, dt_bias, a_log, d_skip, ssd_norm, w_ssd_proj, pool_w, pool_scale,
           w_o, norm_x, w_xq, w_xk, w_xv, w_xo, norm_mlp, w_up, w_down, norm_final):
    bp, seq, _ = x_prompt.shape
    bs = x_sample.shape[0]
    depth = w_in.shape[0]
    assert x_sample.shape[1] == 1 and seq % CHUNK == 0
    hp = x_prompt.reshape(bp * seq, D_MODEL)
    hs = x_sample.reshape(bs, D_MODEL)
    mem = mem_prompt.reshape(bp * MEM_LEN, D_MODEL)
    mk, mv = _kv_proj(mem, w_xk.astype(BF16), w_xv.astype(BF16))
    states = state_ssm.reshape(depth, bs, D_INNER, SSD_STATE)
    ck = _kv_rows(cache_mem_k)
    cv = _kv_rows(cache_mem_v)
    new_states = None
    outs = [[] for _ in range(5)]
    for l in range(depth):
        p = _layer_weights(l, norm_mix, w_in, conv_w, conv_b, dt_bias, a_log, d_skip, ssd_norm, w_ssd_proj,
                           pool_w, pool_scale, w_o, norm_x, w_xq, w_xo, norm_mlp, w_up, w_down)
        hp, sst, cst, pst = _prompt_layer(hp, mk, mv, l, p, bp, seq)
        hs, new_states, ncst, npst = _sample_layer(hs, states, new_states, state_conv[l], state_pool[l],
                                                   ck, cv, l, p)
        outs[0].append(sst.reshape(bp, SSD_HEADS, SSD_HEAD_DIM, SSD_STATE))
        outs[1].append(cst)
        outs[2].append(pst)
        outs[3].append(ncst.reshape(bs, CONV_WIDTH - 1, CONV_DIM))
        outs[4].append(npst.reshape(bs, POOL_BUF, D_MODEL))
    y_prompt = _rmsnorm(hp, norm_final).reshape(bp, seq, D_MODEL)
    y_sample = _rmsnorm(hs, norm_final).reshape(bs, 1, D_MODEL)
    ssm_p, conv_p, pool_p, conv_s, pool_s = (jnp.stack(o) for o in outs)
    kv_shape = (depth, bp, MEM_LEN, X_HEADS, X_HEAD_DIM)
    return (y_prompt, y_sample, ssm_p, conv_p, pool_p, mk.reshape(kv_shape), mv.reshape(kv_shape),
            new_states.reshape(depth, bs, SSD_HEADS, SSD_HEAD_DIM, SSD_STATE), conv_s, pool_s)
```

```python
import functools
import math

import jax
import jax.numpy as jnp
from jax import lax
from jax.experimental import pallas as pl
from jax.experimental.pallas import tpu as pltpu

F32 = jnp.float32
BF16 = jnp.bfloat16

D_MODEL = 1024
D_INNER = 2048
SSD_HEAD_DIM = 64
SSD_HEADS = 32
SSD_GROUPS = 4
SSD_STATE = 128
GROUP_W = D_INNER // SSD_GROUPS
CONV_WIDTH = 4
CONV_DIM = D_INNER + 2 * SSD_GROUPS * SSD_STATE
CHUNK = 128
POOL_WINDOWS = (2, 4, 8, 16)
POOL_GW = 256
POOL_BUF = 15
MEM_LEN = 256
X_HEADS = 4
X_HEAD_DIM = 256
D_FF = 4096
EPS = 1e-6
PAST_LEN = 16384
LOG2E = math.log2(math.e)
OFF_XBC = D_INNER
OFF_DT = OFF_XBC + CONV_DIM
OFF_POOL = OFF_DT + SSD_HEADS
LANES = 128
SUBLANES = 8
XZD_Z = CONV_DIM
XZD_DT = CONV_DIM + D_INNER
XZD_W = XZD_DT + LANES
PG_W = 3 * D_MODEL
VMEM_LIMIT = 52 * 1024 * 1024


def _cparams(*sem):
    return pltpu.CompilerParams(dimension_semantics=sem, vmem_limit_bytes=VMEM_LIMIT)


def _layer_spec(l, tail, resident=False):
    mode = dict(pipeline_mode=pl.Buffered(1)) if resident else {}
    return pl.BlockSpec((None,) + tuple(tail), lambda *_: (l,) + (0,) * len(tail), **mode)


def _dot(a, b):
    return jnp.dot(a, b, preferred_element_type=F32)


def _dot_nt(a, b):
    return lax.dot_general(a, b, (((1,), (1,)), ((), ())), preferred_element_type=F32)


def _split3(x):
    hi = x.astype(BF16)
    r1 = x - hi.astype(F32)
    mid = r1.astype(BF16)
    lo = (r1 - mid.astype(F32)).astype(BF16)
    return hi, mid, lo


def _silu(x):
    return x * jax.nn.sigmoid(x)


def _softplus(x):
    return jnp.maximum(x, 0.0) + jnp.log1p(jnp.exp(-jnp.abs(x)))


def _rows8(v, p8, op):
    n = v.shape[-1]
    return op(v.reshape(-1, SUBLANES, n), p8[None]).reshape(v.shape)


def _mul8(v, p8):
    return _rows8(v, p8, jnp.multiply)


def _add8(v, p8):
    return _rows8(v, p8, jnp.add)


def _rms(x, g):
    return x * lax.rsqrt(jnp.mean(x * x, axis=-1, keepdims=True) + EPS) * g


def _group_rms(y):
    parts = []
    for g in range(SSD_GROUPS):
        yg = y[:, g * GROUP_W:(g + 1) * GROUP_W]
        parts.append(yg * lax.rsqrt(jnp.mean(yg * yg, axis=-1, keepdims=True) + EPS))
    return jnp.concatenate(parts, axis=1)


def _merge(x, ya, d, ga, gb, wsp_ref, pw_ref, ps, wo_ref):
    ssd = _dot(ya, wsp_ref[...])
    pool = jnp.concatenate(
        [_dot(d[:, g * POOL_GW:(g + 1) * POOL_GW], pw_ref[g]) for g in range(len(POOL_WINDOWS))], axis=1)
    merged = jax.nn.sigmoid(ga) * ssd + jax.nn.sigmoid(gb) * ps(pool)
    return x + _dot(merged.astype(BF16), wo_ref[...])


def _mm_body(*refs, norm, has_res, out_scale):
    it = iter(refs)
    x_ref = next(it)
    w_ref = next(it)
    g_ref = next(it) if norm else None
    r_ref = next(it) if has_res else None
    o_ref = next(it)
    u_ref = next(it)

    @pl.when(pl.program_id(1) == 0)
    def _():
        x = x_ref[...].astype(F32)
        if norm:
            x = _rms(x, g_ref[...])
        u_ref[...] = x.astype(BF16)

    acc = _dot(u_ref[...], w_ref[...])
    if out_scale != 1.0:
        acc = acc * out_scale
    if has_res:
        acc = acc + r_ref[...]
    o_ref[...] = acc.astype(o_ref.dtype)


def _mm(x, w, l, *, g=None, res=None, out_scale=1.0, out_dtype=F32, tm=512, tn=1024):
    m, k = x.shape
    n = w.shape[2]
    tm = min(tm, m)
    tn = min(tn, n)
    assert m % tm == 0 and n % tn == 0, (m, n, tm, tn)
    in_specs = [pl.BlockSpec((tm, k), lambda i, j: (i, 0)),
                pl.BlockSpec((None, k, tn), lambda i, j: (l, 0, j))]
    args = [x, w]
    if g is not None:
        in_specs.append(pl.BlockSpec((None, 1, k), lambda i, j: (l, 0, 0)))
        args.append(g)
    if res is not None:
        in_specs.append(pl.BlockSpec((tm, tn), lambda i, j: (i, j)))
        args.append(res)
    return pl.pallas_call(
        functools.partial(_mm_body, norm=g is not None, has_res=res is not None, out_scale=out_scale),
        grid=(m // tm, n // tn),
        in_specs=in_specs,
        out_specs=pl.BlockSpec((tm, tn), lambda i, j: (i, j)),
        out_shape=jax.ShapeDtypeStruct((m, n), out_dtype),
        scratch_shapes=[pltpu.VMEM((tm, k), BF16)],
        compiler_params=_cparams("parallel", "arbitrary"),
        name="mm",
    )(*args)


def _kv_body(x_ref, wk_ref, wv_ref, k_ref, v_ref, u_ref):
    @pl.when(pl.program_id(1) == 0)
    def _():
        u_ref[...] = x_ref[...].astype(BF16)

    k_ref[...] = _dot(u_ref[...], wk_ref[...])
    v_ref[...] = _dot(u_ref[...], wv_ref[...])


def _kv_proj(mem, wk, wv, tm=512):
    m, k = mem.shape
    depth = wk.shape[0]
    tm = min(tm, m)
    wspec = pl.BlockSpec((None, k, D_MODEL), lambda i, l: (l, 0, 0))
    ospec = pl.BlockSpec((None, tm, D_MODEL), lambda i, l: (l, i, 0))
    oshape = jax.ShapeDtypeStruct((depth, m, D_MODEL), F32)
    return pl.pallas_call(
        _kv_body,
        grid=(m // tm, depth),
        in_specs=[pl.BlockSpec((tm, k), lambda i, l: (i, 0)), wspec, wspec],
        out_specs=[ospec, ospec],
        out_shape=[oshape, oshape],
        scratch_shapes=[pltpu.VMEM((tm, k), BF16)],
        compiler_params=_cparams("parallel", "arbitrary"),
        name="kv_proj",
    )(mem, wk, wv)


def _rmsnorm_body(x_ref, g_ref, o_ref):
    o_ref[...] = _rms(x_ref[...], g_ref[...])


def _rmsnorm(x, g, tm=512):
    m, k = x.shape
    tm = min(tm, m)
    return pl.pallas_call(
        _rmsnorm_body,
        grid=(m // tm,),
        in_specs=[pl.BlockSpec((tm, k), lambda i: (i, 0)), pl.BlockSpec((1, k), lambda i: (0, 0))],
        out_specs=pl.BlockSpec((tm, k), lambda i: (i, 0)),
        out_shape=jax.ShapeDtypeStruct((m, k), F32),
        compiler_params=_cparams("parallel"),
        name="rmsnorm",
    )(x, g.reshape(1, k))


PH = CHUNK // SUBLANES
HB = PH + SUBLANES
CONV_WRAP = (5, 6, 7)


def _blk(v, r):
    return v[r * PH:(r + 1) * PH]


def _mixer_body(x_ref, g_ref, wxzd_ref, wpg_ref, cw_ref, cb_ref, dtb_ref, a_ref, de_ref, nw_ref,
                wsp_ref, pw_ref, ps_ref, wo_ref,
                o_ref, sst_ref, cst_ref, pst_ref,
                perm, xh, xsh, ph, psh1, psh2, st_scr, *, nchunks):
    c = pl.program_id(1)
    L = CHUNK

    @pl.when(c == 0)
    def _():
        for q in range(len(CONV_WRAP)):
            xh[q * HB:q * HB + SUBLANES, :] = jnp.zeros((SUBLANES, CONV_DIM), F32)
        for p in range(SUBLANES):
            ph[p * HB:p * HB + SUBLANES, :] = jnp.zeros((SUBLANES, D_MODEL), F32)
        st_scr[...] = jnp.zeros_like(st_scr)

    n_lt = D_MODEL // LANES
    for t in range(n_lt):
        perm[t] = x_ref[:, t * LANES:(t + 1) * LANES]
    x = jnp.concatenate(
        [jnp.concatenate([perm[t, pl.ds(r, PH, stride=SUBLANES), :] for t in range(n_lt)], axis=1)
         for r in range(SUBLANES)], axis=0)
    u = _mul8(x * lax.rsqrt(jnp.mean(x * x, axis=-1, keepdims=True) + EPS), g_ref[...]).astype(BF16)
    xbc = _dot(u, wxzd_ref[:, 0:XZD_Z])
    z = _dot(u, wxzd_ref[:, XZD_Z:XZD_DT])
    dtr = _dot(u, wxzd_ref[:, XZD_DT:XZD_W])
    pu = _dot(u, wpg_ref[:, 0:D_MODEL])
    ga = _dot(u, wpg_ref[:, D_MODEL:2 * D_MODEL])
    gb = _dot(u, wpg_ref[:, 2 * D_MODEL:3 * D_MODEL])

    for q, p in enumerate(CONV_WRAP):
        xh[q * HB + SUBLANES:(q + 1) * HB, :] = _blk(xbc, p)
        xsh[q * PH:(q + 1) * PH, :] = xh[q * HB + SUBLANES - 1:(q + 1) * HB - 1, :]
    conv_blocks = []
    for r in range(SUBLANES):
        acc = None
        for k in range(CONV_WIDTH):
            rp = r - (CONV_WIDTH - 1) + k
            src = _blk(xbc, rp) if rp >= 0 else xsh[(rp + 3) * PH:(rp + 4) * PH, :]
            term = _mul8(src, cw_ref[k])
            acc = _add8(term, cb_ref[...]) if acc is None else acc + term
        conv_blocks.append(acc)
    conv = jnp.concatenate(conv_blocks, axis=0)
    act = _silu(conv)
    xs = act[:, :D_INNER]

    dt = _softplus(_add8(dtr, dtb_ref[...]))
    a = _mul8(dt, a_ref[...])
    j0 = lax.broadcasted_iota(jnp.int32, (L, L), 0)
    j1 = lax.broadcasted_iota(jnp.int32, (L, L), 1)
    tok = lambda j: ((j & (PH - 1)) << 3) | (j >> 4)
    tri = tok(j0) >= tok(j1)
    ltri = jnp.where(tri, 1.0, 0.0).astype(BF16)
    a_hi, a_mid, a_lo = _split3(a)
    cs = (_dot(ltri, a_hi) + _dot(ltri, a_mid) + _dot(ltri, a_lo)) * LOG2E
    cs_t = cs.T
    dt_t = dt.T
    cs_last = cs[L - 1:L, :]
    wdec = dt * jnp.exp2(cs_last - cs)
    lt64 = j1 < SSD_HEAD_DIM
    lt64_row = lax.broadcasted_iota(jnp.int32, (1, L), 1) < SSD_HEAD_DIM

    y_pairs = []
    for g in range(SSD_GROUPS):
        bm = act[:, D_INNER + g * SSD_STATE:D_INNER + (g + 1) * SSD_STATE]
        cm = act[:, D_INNER + (SSD_GROUPS + g) * SSD_STATE:D_INNER + (SSD_GROUPS + g + 1) * SSD_STATE]
        cb = _dot_nt(cm.astype(BF16), bm.astype(BF16))
        xw_pairs = []
        dec_pairs = []
        for kk in range(4):
            k = g * 4 + kk
            lhs_rows, colbs = [], []
            for h in (2 * k, 2 * k + 1):
                colb = jnp.broadcast_to(cs[:, h:h + 1], (L, L))
                lm = jnp.where(tri, jnp.exp2(colb - cs_t[h:h + 1, :]), 0.0)
                lhs_rows.append(jnp.concatenate([cb * lm * dt_t[h:h + 1, :], cm * jnp.exp2(colb)], axis=1))
                colbs.append(colb)
            lhs = jnp.concatenate(lhs_rows, axis=0).astype(BF16)
            xs_p = xs[:, k * LANES:(k + 1) * LANES]
            st_p = st_scr[:, k * LANES:(k + 1) * LANES]
            both = _dot(lhs, jnp.concatenate([xs_p, st_p], axis=0).astype(BF16))
            y_pairs.append(jnp.where(lt64, both[:L], both[L:]))
            wb = jnp.where(lt64, jnp.broadcast_to(wdec[:, 2 * k:2 * k + 1], (L, L)),
                           jnp.broadcast_to(wdec[:, 2 * k + 1:2 * k + 2], (L, L)))
            xw_pairs.append(xs_p * wb)
            dec_pairs.append(jnp.exp2(jnp.where(lt64_row, colbs[0][L - 1:L, :], colbs[1][L - 1:L, :])))
        xw = jnp.concatenate(xw_pairs, axis=1).astype(BF16)
        dec = jnp.concatenate(dec_pairs, axis=1)
        inc = _dot(bm.T.astype(BF16), xw)
        sl = slice(g * GROUP_W, (g + 1) * GROUP_W)
        st_scr[:, sl] = st_scr[:, sl] * dec + inc

    y = jnp.concatenate(y_pairs, axis=1)
    y = y + _mul8(xs, de_ref[...])
    y = y * _silu(z)
    ya = _mul8(_group_rms(y), nw_ref[...]).astype(BF16)

    for p in range(SUBLANES):
        ph[p * HB + SUBLANES:(p + 1) * HB, :] = _blk(pu, p)
    w_max = POOL_WINDOWS[-1]
    lo_max = (len(POOL_WINDOWS) - 1) * POOL_GW
    for p in range(SUBLANES):
        psh1[p * PH:(p + 1) * PH, :] = ph[p * HB + SUBLANES - 1:(p + 1) * HB - 1, :]
        psh2[p * PH:(p + 1) * PH, :] = ph[p * HB + SUBLANES - 2:(p + 1) * HB - 2, lo_max:lo_max + POOL_GW]
    i_col = lax.broadcasted_iota(jnp.int32, (PH, 1), 0)
    d_blocks = []
    for r in range(SUBLANES):
        pos = c * L + i_col * SUBLANES + r
        outs = []
        for gi, w in enumerate(POOL_WINDOWS):
            lo = gi * POOL_GW
            cur = _blk(pu, r)[:, lo:lo + POOL_GW]
            acc = cur
            for k in range(1, w):
                p, shift = (r - k) % SUBLANES, -((r - k) // SUBLANES)
                if shift == 0:
                    src = _blk(pu, p)[:, lo:lo + POOL_GW]
                elif shift == 1:
                    src = psh1[p * PH:(p + 1) * PH, lo:lo + POOL_GW]
                else:
                    assert shift == 2 and w == w_max
                    src = psh2[p * PH:(p + 1) * PH, :]
                acc = acc + src
            outs.append(acc / jnp.minimum(w, pos + 1).astype(F32) - cur)
        d_blocks.append(jnp.concatenate(outs, axis=1))
    d = jnp.concatenate(d_blocks, axis=0).astype(BF16)

    out = _merge(x, ya, d, ga, gb, wsp_ref, pw_ref, lambda v: _mul8(v, ps_ref[...]), wo_ref)
    for r in range(SUBLANES):
        for t in range(n_lt):
            perm[t, pl.ds(r, PH, stride=SUBLANES), :] = _blk(out, r)[:, t * LANES:(t + 1) * LANES]
    for t in range(n_lt):
        o_ref[:, t * LANES:(t + 1) * LANES] = perm[t]

    @pl.when(c == nchunks - 1)
    def _():
        for k in range(D_INNER // LANES):
            sst_ref[0, k * LANES:(k + 1) * LANES, :] = st_scr[:, k * LANES:(k + 1) * LANES].T
        for q in range(len(CONV_WRAP)):
            cst_ref[0, q:q + 1, :] = xh[(q + 1) * HB - 1:(q + 1) * HB, :]
        for n in range(POOL_BUF):
            t = L - POOL_BUF + n
            row = (t % SUBLANES) * HB + SUBLANES + t // SUBLANES
            pst_ref[0, n:n + 1, :] = ph[row:row + 1, :]

    for q in range(len(CONV_WRAP)):
        xh[q * HB:q * HB + SUBLANES, :] = xh[q * HB + PH:(q + 1) * HB, :]
    for p in range(SUBLANES):
        ph[p * HB:p * HB + SUBLANES, :] = ph[p * HB + PH:(p + 1) * HB, :]


def _mixer_prompt(x, W, l, batch, seq):
    nchunks = seq // CHUNK
    tok = lambda b, c: (b * nchunks + c, 0)
    per_batch = lambda *tail: pl.BlockSpec((1,) + tail, lambda b, c: (b,) + (0,) * len(tail))
    return pl.pallas_call(
        functools.partial(_mixer_body, nchunks=nchunks),
        grid=(batch, nchunks),
        in_specs=[pl.BlockSpec((CHUNK, D_MODEL), tok),
                  _layer_spec(l, (SUBLANES, D_MODEL)),
                  _layer_spec(l, (D_MODEL, XZD_W), resident=True),
                  _layer_spec(l, (D_MODEL, PG_W), resident=True),
                  _layer_spec(l, (CONV_WIDTH, SUBLANES, CONV_DIM)),
                  _layer_spec(l, (SUBLANES, CONV_DIM)),
                  _layer_spec(l, (SUBLANES, LANES)),
                  _layer_spec(l, (SUBLANES, LANES)),
                  _layer_spec(l, (SUBLANES, D_INNER)),
                  _layer_spec(l, (SUBLANES, D_INNER)),
                  _layer_spec(l, (D_INNER, D_MODEL), resident=True),
                  _layer_spec(l, (len(POOL_WINDOWS), POOL_GW, POOL_GW), resident=True),
                  _layer_spec(l, (SUBLANES, D_MODEL)),
                  _layer_spec(l, (D_MODEL, D_MODEL), resident=True)],
        out_specs=[pl.BlockSpec((CHUNK, D_MODEL), tok),
                   per_batch(D_INNER, SSD_STATE),
                   per_batch(CONV_WIDTH - 1, CONV_DIM),
                   per_batch(POOL_BUF, D_MODEL)],
        out_shape=[jax.ShapeDtypeStruct((batch * seq, D_MODEL), F32),
                   jax.ShapeDtypeStruct((batch, D_INNER, SSD_STATE), F32),
                   jax.ShapeDtypeStruct((batch, CONV_WIDTH - 1, CONV_DIM), F32),
                   jax.ShapeDtypeStruct((batch, POOL_BUF, D_MODEL), F32)],
        scratch_shapes=[pltpu.VMEM((D_MODEL // LANES, CHUNK, LANES), F32),
                        pltpu.VMEM((len(CONV_WRAP) * HB, CONV_DIM), F32),
                        pltpu.VMEM((len(CONV_WRAP) * PH, CONV_DIM), F32),
                        pltpu.VMEM((SUBLANES * HB, D_MODEL), F32),
                        pltpu.VMEM((SUBLANES * PH, D_MODEL), F32),
                        pltpu.VMEM((SUBLANES * PH, POOL_GW), F32),
                        pltpu.VMEM((SSD_STATE, D_INNER), F32)],
        compiler_params=_cparams("parallel", "arbitrary"),
        name="mixer_prompt",
    )(x, W["norm_mix8"], W["w_xzd"], W["w_pg"], W["conv_w8"], W["conv_b8"], W["dt_bias8"], W["a_neg8"],
      W["d_e8"], W["ssd_norm8"], W["w_ssd_proj"], W["pool_w"], W["pool_scale8"], W["w_o"])


def _attn_body(x_ref, g_ref, wq_ref, k_ref, v_ref, wo_ref, o_ref):
    x = x_ref[...]
    u = _rms(x, g_ref[...]).astype(BF16)
    q = (_dot(u, wq_ref[...]) * (X_HEAD_DIM ** -0.5)).astype(BF16)
    outs = []
    for h in range(X_HEADS):
        sl = slice(h * X_HEAD_DIM, (h + 1) * X_HEAD_DIM)
        s = _dot_nt(q[:, sl], k_ref[:, sl].astype(BF16))
        p = jnp.exp(s - jnp.max(s, axis=-1, keepdims=True))
        p = p / jnp.sum(p, axis=-1, keepdims=True)
        outs.append(_dot(p.astype(BF16), v_ref[:, sl].astype(BF16)))
    o = jnp.concatenate(outs, axis=1).astype(BF16)
    o_ref[...] = x + _dot(o, wo_ref[...])


def _attn_prompt(x, mk, mv, W, l, batch, seq, tm=512):
    tm = min(tm, seq)
    assert seq % tm == 0
    nt = seq // tm
    tok = lambda b, i: (b * nt + i, 0)
    kv = pl.BlockSpec((None, MEM_LEN, D_MODEL), lambda b, i: (l, b, 0))
    return pl.pallas_call(
        _attn_body,
        grid=(batch, nt),
        in_specs=[pl.BlockSpec((tm, D_MODEL), tok),
                  _layer_spec(l, (1, D_MODEL)),
                  _layer_spec(l, (D_MODEL, D_MODEL), resident=True), kv, kv,
                  _layer_spec(l, (D_MODEL, D_MODEL), resident=True)],
        out_specs=pl.BlockSpec((tm, D_MODEL), tok),
        out_shape=jax.ShapeDtypeStruct((batch * seq, D_MODEL), F32),
        compiler_params=_cparams("parallel", "parallel"),
        name="attn_prompt",
    )(x, W["norm_x"], W["w_xq"], mk, mv, W["w_xo"])


def _mlp_body(x_ref, g_ref, wu_ref, wd_ref, o_ref, xm_ref):
    @pl.when(pl.program_id(1) == 0)
    def _():
        x = x_ref[...]
        xm_ref[...] = _rms(x, g_ref[...]).astype(BF16)
        o_ref[...] = x

    h = jnp.square(jnp.maximum(_dot(xm_ref[...], wu_ref[...]), 0.0)).astype(BF16)
    o_ref[...] += _dot(h, wd_ref[...])


def _mlp(x, W, l, tm=1024, tf=1024):
    m = x.shape[0]
    tm = min(tm, m)
    assert m % tm == 0
    return pl.pallas_call(
        _mlp_body,
        grid=(m // tm, D_FF // tf),
        in_specs=[pl.BlockSpec((tm, D_MODEL), lambda i, k: (i, 0)),
                  _layer_spec(l, (1, D_MODEL)),
                  pl.BlockSpec((None, D_MODEL, tf), lambda i, k: (l, 0, k)),
                  pl.BlockSpec((None, tf, D_MODEL), lambda i, k: (l, k, 0))],
        out_specs=pl.BlockSpec((tm, D_MODEL), lambda i, k: (i, 0)),
        out_shape=jax.ShapeDtypeStruct((m, D_MODEL), F32),
        scratch_shapes=[pltpu.VMEM((tm, D_MODEL), BF16)],
        compiler_params=_cparams("parallel", "arbitrary"),
        name="mlp",
    )(x, W["norm_mlp"], W["w_up"], W["w_down"])


STEP_TB = 32


def _step_body(x_ref, g_ref, w_ref, wpg_ref, cst_ref, cw_ref, cb_ref, dtb_ref, a_ref, pst_ref,
               ncst_ref, npst_ref, act_ref, xdt_ref, dec_ref, d_ref, z_ref, pg_ref, *, start_pos):
    x = x_ref[...]
    u = _mul8(x * lax.rsqrt(jnp.mean(x * x, axis=-1, keepdims=True) + EPS), g_ref[...]).astype(BF16)
    xbc = _dot(u, w_ref[:, 0:XZD_Z])
    z_ref[...] = _dot(u, w_ref[:, XZD_Z:XZD_DT])
    dtr = _dot(u, w_ref[:, XZD_DT:XZD_W])
    pg = _dot(u, wpg_ref[...])
    pg_ref[...] = pg
    pu = pg[:, 0:D_MODEL]

    conv = None
    for k in range(CONV_WIDTH):
        src = cst_ref[:, k * CONV_DIM:(k + 1) * CONV_DIM] if k < CONV_WIDTH - 1 else xbc
        term = _mul8(src, cw_ref[k])
        conv = _add8(term, cb_ref[...]) if conv is None else conv + term
    for k in range(CONV_WIDTH - 2):
        ncst_ref[:, k * CONV_DIM:(k + 1) * CONV_DIM] = cst_ref[:, (k + 1) * CONV_DIM:(k + 2) * CONV_DIM]
    ncst_ref[:, (CONV_WIDTH - 2) * CONV_DIM:] = xbc
    act = _silu(conv)
    act_ref[...] = act

    dt = _softplus(_add8(dtr, dtb_ref[...]))
    dec = jnp.exp(_mul8(dt, a_ref[...]))
    lt64 = lax.broadcasted_iota(jnp.int32, (1, LANES), 1) < SSD_HEAD_DIM
    m = dt.shape[0]
    for k in range(D_INNER // LANES):
        dt_e = jnp.where(lt64, jnp.broadcast_to(dt[:, 2 * k:2 * k + 1], (m, LANES)),
                         jnp.broadcast_to(dt[:, 2 * k + 1:2 * k + 2], (m, LANES)))
        dec_e = jnp.where(lt64, jnp.broadcast_to(dec[:, 2 * k:2 * k + 1], (m, LANES)),
                          jnp.broadcast_to(dec[:, 2 * k + 1:2 * k + 2], (m, LANES)))
        xdt_ref[:, k * LANES:(k + 1) * LANES] = act[:, k * LANES:(k + 1) * LANES] * dt_e
        dec_ref[:, k * LANES:(k + 1) * LANES] = dec_e

    outs = []
    for gi, w in enumerate(POOL_WINDOWS):
        lo = gi * POOL_GW
        cur = pu[:, lo:lo + POOL_GW]
        acc = cur
        for k in range(1, w):
            j = POOL_BUF - k
            acc = acc + pst_ref[:, j * D_MODEL + lo:j * D_MODEL + lo + POOL_GW]
        outs.append(acc / float(min(w, start_pos + 1)) - cur)
    d_ref[...] = jnp.concatenate(outs, axis=1).astype(d_ref.dtype)
    npst_ref[:, :(POOL_BUF - 1) * D_MODEL] = pst_ref[:, D_MODEL:]
    npst_ref[:, (POOL_BUF - 1) * D_MODEL:] = pu


def _sample_step(x, cst, pst, W, l, start_pos):
    m = x.shape[0]
    tm = min(STEP_TB, m)
    rows = lambda width: pl.BlockSpec((tm, width), lambda i: (i, 0))
    widths = [(CONV_WIDTH - 1) * CONV_DIM, POOL_BUF * D_MODEL, CONV_DIM, D_INNER, D_INNER, D_MODEL, D_INNER, PG_W]
    dtypes = [F32, F32, F32, F32, F32, BF16, F32, F32]
    return pl.pallas_call(
        functools.partial(_step_body, start_pos=start_pos),
        grid=(m // tm,),
        in_specs=[rows(D_MODEL), _layer_spec(l, (SUBLANES, D_MODEL)),
                  _layer_spec(l, (D_MODEL, XZD_W), resident=True), _layer_spec(l, (D_MODEL, PG_W), resident=True),
                  rows((CONV_WIDTH - 1) * CONV_DIM), _layer_spec(l, (CONV_WIDTH, SUBLANES, CONV_DIM)),
                  _layer_spec(l, (SUBLANES, CONV_DIM)), _layer_spec(l, (SUBLANES, LANES)),
                  _layer_spec(l, (SUBLANES, LANES)), rows(POOL_BUF * D_MODEL)],
        out_specs=[rows(w) for w in widths],
        out_shape=[jax.ShapeDtypeStruct((m, w), dt) for w, dt in zip(widths, dtypes)],
        compiler_params=_cparams("parallel"),
        name="sample_step",
    )(x, W["norm_mix8"], W["w_xzd"], W["w_pg"], cst, W["conv_w8"], W["conv_b8"], W["dt_bias8"], W["a_neg8"], pst)


SSM_TB = 8


def _ssm_body(s_ref, xdt_ref, dec_ref, act_ref, *rest):
    so_ref, y_ref = rest[-2:]
    for j in range(SSM_TB):
        for g in range(SSD_GROUPS):
            rows = slice(g * GROUP_W, (g + 1) * GROUP_W)
            b_row = act_ref[j:j + 1, D_INNER + g * SSD_STATE:D_INNER + (g + 1) * SSD_STATE]
            c_row = act_ref[j:j + 1, D_INNER + (SSD_GROUPS + g) * SSD_STATE:
                            D_INNER + (SSD_GROUPS + g + 1) * SSD_STATE]
            sn = s_ref[j, rows, :] * dec_ref[0, rows, j:j + 1] + xdt_ref[0, rows, j:j + 1] * b_row
            so_ref[j, rows, :] = sn
            c8 = jnp.broadcast_to(c_row, (SUBLANES, SSD_STATE)).astype(BF16)
            y_ref[j:j + 1, rows] = _dot_nt(c8, sn.astype(BF16))[0:1, :]


def _sample_ssm(states, l, prev, xdt_c, dec_c, act):
    depth, m = states.shape[:2]
    nb = m // SSM_TB
    st_spec = pl.BlockSpec((None, SSM_TB, D_INNER, SSD_STATE), lambda i: (l, i, 0, 0))
    col_spec = pl.BlockSpec((1, D_INNER, SSM_TB), lambda i: (i, 0, 0))
    in_specs = [st_spec, col_spec, col_spec, pl.BlockSpec((SSM_TB, CONV_DIM), lambda i: (i, 0))]
    args = [states, xdt_c, dec_c, act]
    aliases = {}
    if prev is not None:
        in_specs.append(pl.BlockSpec(memory_space=pl.ANY))
        args.append(prev)
        aliases = {len(args) - 1: 0}
    return pl.pallas_call(
        _ssm_body,
        grid=(nb,),
        in_specs=in_specs,
        out_specs=[st_spec, pl.BlockSpec((SSM_TB, D_INNER), lambda i: (i, 0))],
        out_shape=[jax.ShapeDtypeStruct(states.shape, F32),
                   jax.ShapeDtypeStruct((m, D_INNER), F32)],
        input_output_aliases=aliases,
        compiler_params=_cparams("parallel"),
        name="sample_ssm",
    )(*args)


def _gate_body(y_ref, act_ref, z_ref, de_ref, nw_ref, o_ref):
    y = y_ref[...] + _mul8(act_ref[:, :D_INNER], de_ref[...])
    y = y * _silu(z_ref[...])
    o_ref[...] = _mul8(_group_rms(y), nw_ref[...]).astype(o_ref.dtype)


def _sample_gate(y, act, z, W, l):
    m = y.shape[0]
    full = lambda *shape: pl.BlockSpec(shape, lambda i: (0,) * len(shape))
    return pl.pallas_call(
        _gate_body,
        grid=(1,),
        in_specs=[full(m, D_INNER), full(m, CONV_DIM), full(m, D_INNER),
                  _layer_spec(l, (SUBLANES, D_INNER)), _layer_spec(l, (SUBLANES, D_INNER))],
        out_specs=full(m, D_INNER),
        out_shape=jax.ShapeDtypeStruct((m, D_INNER), BF16),
        compiler_params=_cparams("arbitrary"),
        name="sample_gate",
    )(y, act, z, W["d_e8"], W["ssd_norm8"])


def _mixs_body(ya_ref, d_ref, ga_ref, gb_ref, x_ref, wsp_ref, pw_ref, ps_ref, wo_ref, o_ref):
    o_ref[...] = _merge(x_ref[...], ya_ref[...], d_ref[...], ga_ref[...], gb_ref[...],
                        wsp_ref, pw_ref, lambda v: _mul8(v, ps_ref[...]), wo_ref)


def _mix_sample(ya, d, pg, x, W, l):
    m = x.shape[0]
    blk = lambda width, j: pl.BlockSpec((m, width), lambda i: (0, j))
    return pl.pallas_call(
        _mixs_body,
        grid=(1,),
        in_specs=[blk(D_INNER, 0), blk(D_MODEL, 0), blk(D_MODEL, 1), blk(D_MODEL, 2), blk(D_MODEL, 0),
                  _layer_spec(l, (D_INNER, D_MODEL)), _layer_spec(l, (len(POOL_WINDOWS), POOL_GW, POOL_GW)),
                  _layer_spec(l, (SUBLANES, D_MODEL)), _layer_spec(l, (D_MODEL, D_MODEL))],
        out_specs=blk(D_MODEL, 0),
        out_shape=jax.ShapeDtypeStruct((m, D_MODEL), F32),
        compiler_params=_cparams("arbitrary"),
        name="mix_sample",
    )(ya, d, pg, pg, x, W["w_ssd_proj"], W["pool_w"], W["pool_scale8"], W["w_o"])


ATT_TB = 8
KV_ROWS = MEM_LEN * SUBLANES
HALVES = X_HEAD_DIM // LANES
assert HALVES * X_HEADS == SUBLANES


def _sattn_body(q_ref, k_ref, v_ref, o_ref):
    for j in range(ATT_TB):
        k3 = k_ref[j].reshape(MEM_LEN, SUBLANES, LANES)
        v3 = v_ref[j].reshape(MEM_LEN, SUBLANES, LANES)
        prod = k3 * q_ref[j][None]
        prod = prod + pltpu.roll(prod, X_HEADS, axis=1)
        s = jnp.sum(prod, axis=-1, keepdims=True)
        e = jnp.exp(s - jnp.max(s, axis=0, keepdims=True))
        pr = e / jnp.sum(e, axis=0, keepdims=True)
        o_ref[j] = jnp.sum(pr * v3, axis=0)


def _kv_rows(c):
    depth, m = c.shape[:2]
    c = c.reshape(depth, m, MEM_LEN, X_HEADS, HALVES, LANES).transpose(0, 1, 2, 4, 3, 5)
    return c.reshape(depth, m, KV_ROWS, LANES)


def _sample_attn(q, ck, cv, l):
    m = q.shape[0]
    q8 = q.reshape(m, X_HEADS, HALVES, LANES).transpose(0, 2, 1, 3).reshape(m, SUBLANES, LANES)
    kv = pl.BlockSpec((None, ATT_TB, KV_ROWS, LANES), lambda i: (l, i, 0, 0))
    qo = pl.BlockSpec((ATT_TB, SUBLANES, LANES), lambda i: (i, 0, 0))
    o8 = pl.pallas_call(
        _sattn_body,
        grid=(m // ATT_TB,),
        in_specs=[qo, kv, kv],
        out_specs=qo,
        out_shape=jax.ShapeDtypeStruct((m, SUBLANES, LANES), F32),
        compiler_params=_cparams("parallel"),
        name="sample_attn",
    )(q8, ck, cv)
    return o8.reshape(m, HALVES, X_HEADS, LANES).transpose(0, 2, 1, 3).reshape(m, D_MODEL)


def _prep_weights(norm_mix, w_in, conv_w, conv_b, dt_bias, a_log, d_skip, ssd_norm, w_ssd_proj, pool_w,
                  pool_scale, w_o, norm_x, w_xq, w_xk, w_xv, w_xo, norm_mlp, w_up, w_down):
    depth = w_in.shape[0]
    pad = jnp.zeros((depth, D_MODEL, LANES - SSD_HEADS), F32)
    lane_pad = lambda v: jnp.concatenate([v, jnp.zeros((depth, LANES - SSD_HEADS), F32)], axis=1)
    rep8 = lambda v: jnp.broadcast_to(v[:, None, :], (depth, SUBLANES, v.shape[-1]))
    row = lambda v: v[:, None, :]
    return dict(
        norm_mix8=rep8(norm_mix),
        w_xzd=jnp.concatenate([w_in[:, :, OFF_XBC:OFF_DT], w_in[:, :, :OFF_XBC], w_in[:, :, OFF_DT:OFF_POOL], pad],
                              axis=2).astype(BF16),
        w_pg=w_in[:, :, OFF_POOL:].astype(BF16),
        conv_w8=jnp.broadcast_to(conv_w[:, :, None, :], (depth, CONV_WIDTH, SUBLANES, CONV_DIM)),
        conv_b8=rep8(conv_b),
        dt_bias8=rep8(lane_pad(dt_bias)),
        a_neg8=rep8(lane_pad(-jnp.exp(a_log))),
        d_e8=rep8(jnp.repeat(d_skip, SSD_HEAD_DIM, axis=1)),
        ssd_norm8=rep8(ssd_norm),
        w_ssd_proj=w_ssd_proj.astype(BF16),
        pool_w=pool_w.astype(BF16), pool_scale8=rep8(pool_scale),
        w_o=w_o.astype(BF16), norm_x=row(norm_x), w_xq=w_xq.astype(BF16),
        w_xk=w_xk.astype(BF16), w_xv=w_xv.astype(BF16), w_xo=w_xo.astype(BF16),
        norm_mlp=row(norm_mlp), w_up=w_up.astype(BF16), w_down=w_down.astype(BF16))


def _prompt_layer(x, mk, mv, W, l, batch, seq):
    x, sst, cst, pst = _mixer_prompt(x, W, l, batch, seq)
    x = _attn_prompt(x, mk, mv, W, l, batch, seq)
    x = _mlp(x, W, l)
    return x, sst, cst, pst


def _sample_layer(x, states, new_states, st_conv, st_pool, ck, cv, W, l):
    m = x.shape[0]
    nb = m // SSM_TB
    ncst, npst, act, xdt, dec, d, z, pg = _sample_step(x, st_conv.reshape(m, -1), st_pool.reshape(m, -1), W, l,
                                                       PAST_LEN)
    cols = lambda v: v.reshape(nb, SSM_TB, D_INNER).transpose(0, 2, 1)
    new_states, y = _sample_ssm(states, l, new_states, cols(xdt), cols(dec), act)
    ya = _sample_gate(y, act, z, W, l)
    x = _mix_sample(ya, d, pg, x, W, l)
    q = _mm(x, W["w_xq"], l, g=W["norm_x"], out_scale=X_HEAD_DIM ** -0.5)
    o = _sample_attn(q, ck, cv, l)
    x = _mm(o, W["w_xo"], l, res=x)
    x = _mlp(x, W, l)
    return x, new_states, ncst, npst


def kernel(x_prompt, x_sample, mem_prompt, state_ssm, state_conv, state_pool, cache_mem_k, cache_mem_v,
           norm_mix, w_in, conv_w, conv_b, dt_bias, a_log, d_skip, ssd_norm, w_ssd_proj, pool_w, pool_scale,
           w_o, norm_x, w_xq, w_xk, w_xv, w_xo, norm_mlp, w_up, w_down, norm_final):
    bp, seq, _ = x_prompt.shape
    bs = x_sample.shape[0]
    depth = w_in.shape[0]
    assert x_sample.shape[1] == 1 and seq % CHUNK == 0
    W = _prep_weights(norm_mix, w_in, conv_w, conv_b, dt_bias, a_log, d_skip, ssd_norm, w_ssd_proj, pool_w,
                      pool_scale, w_o, norm_x, w_xq, w_xk, w_xv, w_xo, norm_mlp, w_up, w_down)
    hp = x_prompt.reshape(bp * seq, D_MODEL)
    hs = x_sample.reshape(bs, D_MODEL)
    mem = mem_prompt.reshape(bp * MEM_LEN, D_MODEL)
    mk, mv = _kv_proj(mem, W["w_xk"], W["w_xv"])
    states = state_ssm.reshape(depth, bs, D_INNER, SSD_STATE)
    ck = _kv_rows(cache_mem_k)
    cv = _kv_rows(cache_mem_v)
    new_states = None
    outs = [[] for _ in range(5)]
    for l in range(depth):
        hp, sst, cst, pst = _prompt_layer(hp, mk, mv, W, l, bp, seq)
        hs, new_states, ncst, npst = _sample_layer(hs, states, new_states, state_conv[l], state_pool[l],
                                                   ck, cv, W, l)
        outs[0].append(sst.reshape(bp, SSD_HEADS, SSD_HEAD_DIM, SSD_STATE))
        outs[1].append(cst)
        outs[2].append(pst)
        outs[3].append(ncst.reshape(bs, CONV_WIDTH - 1, CONV_DIM))
        outs[4].append(npst.reshape(bs, POOL_BUF, D_MODEL))
    y_prompt = _rmsnorm(hp, norm_final).reshape(bp, seq, D_MODEL)
    y_sample = _rmsnorm(hs, norm_final).reshape(bs, 1, D_MODEL)
    ssm_p, conv_p, pool_p, conv_s, pool_s = (jnp.stack(o) for o in outs)
    kv_shape = (depth, bp, MEM_LEN, X_HEADS, X_HEAD_DIM)
    return (y_prompt, y_sample, ssm_p, conv_p, pool_p, mk.reshape(kv_shape), mv.reshape(kv_shape),
            new_states.reshape(depth, bs, SSD_HEADS, SSD_HEAD_DIM, SSD_STATE), conv_s, pool_s)
```

```python
import functools
import math

import jax
import jax.numpy as jnp
from jax import lax
from jax.experimental import pallas as pl
from jax.experimental.pallas import tpu as pltpu

F32 = jnp.float32
BF16 = jnp.bfloat16

D_MODEL = 1024
D_INNER = 2048
SSD_HEAD_DIM = 64
SSD_HEADS = 32
SSD_GROUPS = 4
SSD_STATE = 128
GROUP_W = D_INNER // SSD_GROUPS
CONV_WIDTH = 4
CONV_DIM = D_INNER + 2 * SSD_GROUPS * SSD_STATE
CHUNK = 128
POOL_WINDOWS = (2, 4, 8, 16)
POOL_GW = 256
POOL_BUF = 15
MEM_LEN = 256
X_HEADS = 4
X_HEAD_DIM = 256
D_FF = 4096
EPS = 1e-6
PAST_LEN = 16384
LOG2E = math.log2(math.e)
OFF_XBC = D_INNER
OFF_DT = OFF_XBC + CONV_DIM
OFF_POOL = OFF_DT + SSD_HEADS
LANES = 128
SUBLANES = 8
ZXD_W = OFF_DT + LANES
PG_W = 3 * D_MODEL
VMEM_LIMIT = 52 * 1024 * 1024


def _cparams(*sem):
    return pltpu.CompilerParams(dimension_semantics=sem, vmem_limit_bytes=VMEM_LIMIT)


def _layer_spec(l, tail, resident=False):
    mode = dict(pipeline_mode=pl.Buffered(1)) if resident else {}
    return pl.BlockSpec((None,) + tuple(tail), lambda *_: (l,) + (0,) * len(tail), **mode)


def _dot(a, b):
    return jnp.dot(a, b, preferred_element_type=F32)


def _dot_nt(a, b):
    return lax.dot_general(a, b, (((1,), (1,)), ((), ())), preferred_element_type=F32)


def _split3(x):
    hi = x.astype(BF16)
    r1 = x - hi.astype(F32)
    mid = r1.astype(BF16)
    lo = (r1 - mid.astype(F32)).astype(BF16)
    return hi, mid, lo


def _silu(x):
    return x * jax.nn.sigmoid(x)


def _softplus(x):
    return jnp.maximum(x, 0.0) + jnp.log1p(jnp.exp(-jnp.abs(x)))


def _rows8(v, p8, op):
    n = v.shape[-1]
    return op(v.reshape(-1, SUBLANES, n), p8[None]).reshape(v.shape)


def _mul8(v, p8):
    return _rows8(v, p8, jnp.multiply)


def _add8(v, p8):
    return _rows8(v, p8, jnp.add)


def _rms(x, g):
    return x * lax.rsqrt(jnp.mean(x * x, axis=-1, keepdims=True) + EPS) * g


def _group_rms(y):
    parts = []
    for g in range(SSD_GROUPS):
        yg = y[:, g * GROUP_W:(g + 1) * GROUP_W]
        parts.append(yg * lax.rsqrt(jnp.mean(yg * yg, axis=-1, keepdims=True) + EPS))
    return jnp.concatenate(parts, axis=1)


def _merge(x, ya, d, ga, gb, wsp_ref, pw_ref, ps, wo_ref):
    ssd = _dot(ya, wsp_ref[...])
    pool = jnp.concatenate(
        [_dot(d[:, g * POOL_GW:(g + 1) * POOL_GW], pw_ref[g]) for g in range(len(POOL_WINDOWS))], axis=1)
    merged = jax.nn.sigmoid(ga) * ssd + jax.nn.sigmoid(gb) * ps(pool)
    return x + _dot(merged.astype(BF16), wo_ref[...])


def _mm_body(*refs, norm, has_res, out_scale):
    it = iter(refs)
    x_ref = next(it)
    w_ref = next(it)
    g_ref = next(it) if norm else None
    r_ref = next(it) if has_res else None
    o_ref = next(it)
    u_ref = next(it)

    @pl.when(pl.program_id(1) == 0)
    def _():
        x = x_ref[...].astype(F32)
        if norm:
            x = _rms(x, g_ref[...])
        u_ref[...] = x.astype(BF16)

    acc = _dot(u_ref[...], w_ref[...])
    if out_scale != 1.0:
        acc = acc * out_scale
    if has_res:
        acc = acc + r_ref[...]
    o_ref[...] = acc.astype(o_ref.dtype)


def _mm(x, w, l, *, g=None, res=None, out_scale=1.0, out_dtype=F32, tm=512, tn=1024):
    m, k = x.shape
    n = w.shape[2]
    tm = min(tm, m)
    tn = min(tn, n)
    assert m % tm == 0 and n % tn == 0, (m, n, tm, tn)
    in_specs = [pl.BlockSpec((tm, k), lambda i, j: (i, 0)),
                pl.BlockSpec((None, k, tn), lambda i, j: (l, 0, j))]
    args = [x, w]
    if g is not None:
        in_specs.append(pl.BlockSpec((None, 1, k), lambda i, j: (l, 0, 0)))
        args.append(g)
    if res is not None:
        in_specs.append(pl.BlockSpec((tm, tn), lambda i, j: (i, j)))
        args.append(res)
    return pl.pallas_call(
        functools.partial(_mm_body, norm=g is not None, has_res=res is not None, out_scale=out_scale),
        grid=(m // tm, n // tn),
        in_specs=in_specs,
        out_specs=pl.BlockSpec((tm, tn), lambda i, j: (i, j)),
        out_shape=jax.ShapeDtypeStruct((m, n), out_dtype),
        scratch_shapes=[pltpu.VMEM((tm, k), BF16)],
        compiler_params=_cparams("parallel", "arbitrary"),
        name="mm",
    )(*args)


def _kv_body(x_ref, wk_ref, wv_ref, k_ref, v_ref, u_ref):
    @pl.when(pl.program_id(1) == 0)
    def _():
        u_ref[...] = x_ref[...].astype(BF16)

    k_ref[...] = _dot(u_ref[...], wk_ref[...])
    v_ref[...] = _dot(u_ref[...], wv_ref[...])


def _kv_proj(mem, wk, wv, tm=512):
    m, k = mem.shape
    depth = wk.shape[0]
    tm = min(tm, m)
    wspec = pl.BlockSpec((None, k, D_MODEL), lambda i, l: (l, 0, 0))
    ospec = pl.BlockSpec((None, tm, D_MODEL), lambda i, l: (l, i, 0))
    oshape = jax.ShapeDtypeStruct((depth, m, D_MODEL), F32)
    return pl.pallas_call(
        _kv_body,
        grid=(m // tm, depth),
        in_specs=[pl.BlockSpec((tm, k), lambda i, l: (i, 0)), wspec, wspec],
        out_specs=[ospec, ospec],
        out_shape=[oshape, oshape],
        scratch_shapes=[pltpu.VMEM((tm, k), BF16)],
        compiler_params=_cparams("parallel", "arbitrary"),
        name="kv_proj",
    )(mem, wk, wv)


def _rmsnorm_body(x_ref, g_ref, o_ref):
    o_ref[...] = _rms(x_ref[...], g_ref[...])


def _rmsnorm(x, g, tm=512):
    m, k = x.shape
    tm = min(tm, m)
    return pl.pallas_call(
        _rmsnorm_body,
        grid=(m // tm,),
        in_specs=[pl.BlockSpec((tm, k), lambda i: (i, 0)), pl.BlockSpec((1, k), lambda i: (0, 0))],
        out_specs=pl.BlockSpec((tm, k), lambda i: (i, 0)),
        out_shape=jax.ShapeDtypeStruct((m, k), F32),
        compiler_params=_cparams("parallel"),
        name="rmsnorm",
    )(x, g.reshape(1, k))


PH = CHUNK // SUBLANES
HB = PH + SUBLANES
CONV_WRAP = (5, 6, 7)


def _blk(v, r):
    return v[r * PH:(r + 1) * PH]


def _mixer_body(x_ref, g_ref, wzxd_ref, wpg_ref, cw_ref, cb_ref, dtb_ref, a_ref, de_ref, nw_ref,
                wsp_ref, pw_ref, ps_ref, wo_ref,
                o_ref, sst_ref, cst_ref, pst_ref,
                perm, xh, xsh, ph, psh1, psh2, st_scr, *, nchunks):
    c = pl.program_id(1)
    L = CHUNK

    @pl.when(c == 0)
    def _():
        for q in range(len(CONV_WRAP)):
            xh[q * HB:q * HB + SUBLANES, :] = jnp.zeros((SUBLANES, CONV_DIM), F32)
        for p in range(SUBLANES):
            ph[p * HB:p * HB + SUBLANES, :] = jnp.zeros((SUBLANES, D_MODEL), F32)
        st_scr[...] = jnp.zeros_like(st_scr)

    n_lt = D_MODEL // LANES
    for t in range(n_lt):
        perm[t] = x_ref[:, t * LANES:(t + 1) * LANES]
    x = jnp.concatenate(
        [jnp.concatenate([perm[t, pl.ds(r, PH, stride=SUBLANES), :] for t in range(n_lt)], axis=1)
         for r in range(SUBLANES)], axis=0)
    u = _mul8(x * lax.rsqrt(jnp.mean(x * x, axis=-1, keepdims=True) + EPS), g_ref[...]).astype(BF16)
    z = _dot(u, wzxd_ref[:, 0:OFF_XBC])
    xbc = _dot(u, wzxd_ref[:, OFF_XBC:OFF_DT])
    dtr = _dot(u, wzxd_ref[:, OFF_DT:ZXD_W])
    pu = _dot(u, wpg_ref[:, 0:D_MODEL])
    ga = _dot(u, wpg_ref[:, D_MODEL:2 * D_MODEL])
    gb = _dot(u, wpg_ref[:, 2 * D_MODEL:3 * D_MODEL])

    for q, p in enumerate(CONV_WRAP):
        xh[q * HB + SUBLANES:(q + 1) * HB, :] = _blk(xbc, p)
        xsh[q * PH:(q + 1) * PH, :] = xh[q * HB + SUBLANES - 1:(q + 1) * HB - 1, :]
    conv_blocks = []
    for r in range(SUBLANES):
        acc = None
        for k in range(CONV_WIDTH):
            rp = r - (CONV_WIDTH - 1) + k
            src = _blk(xbc, rp) if rp >= 0 else xsh[(rp + 3) * PH:(rp + 4) * PH, :]
            term = _mul8(src, cw_ref[k])
            acc = _add8(term, cb_ref[...]) if acc is None else acc + term
        conv_blocks.append(acc)
    conv = jnp.concatenate(conv_blocks, axis=0)
    act = _silu(conv)
    xs = act[:, :D_INNER]

    dt = _softplus(_add8(dtr, dtb_ref[...]))
    a = _mul8(dt, a_ref[...])
    j0 = lax.broadcasted_iota(jnp.int32, (L, L), 0)
    j1 = lax.broadcasted_iota(jnp.int32, (L, L), 1)
    tok = lambda j: ((j & (PH - 1)) << 3) | (j >> 4)
    tri = tok(j0) >= tok(j1)
    ltri = jnp.where(tri, 1.0, 0.0).astype(BF16)
    a_hi, a_mid, a_lo = _split3(a)
    cs = (_dot(ltri, a_hi) + _dot(ltri, a_mid) + _dot(ltri, a_lo)) * LOG2E
    cs_t = cs.T
    dt_t = dt.T
    cs_last = cs[L - 1:L, :]
    wdec = dt * jnp.exp2(cs_last - cs)
    lt64 = j1 < SSD_HEAD_DIM
    lt64_row = lax.broadcasted_iota(jnp.int32, (1, L), 1) < SSD_HEAD_DIM

    y_pairs = []
    for g in range(SSD_GROUPS):
        bm = act[:, D_INNER + g * SSD_STATE:D_INNER + (g + 1) * SSD_STATE]
        cm = act[:, D_INNER + (SSD_GROUPS + g) * SSD_STATE:D_INNER + (SSD_GROUPS + g + 1) * SSD_STATE]
        cb = _dot_nt(cm.astype(BF16), bm.astype(BF16))
        xw_pairs = []
        dec_pairs = []
        for kk in range(4):
            k = g * 4 + kk
            lhs_rows, colbs = [], []
            for h in (2 * k, 2 * k + 1):
                colb = jnp.broadcast_to(cs[:, h:h + 1], (L, L))
                lm = jnp.where(tri, jnp.exp2(colb - cs_t[h:h + 1, :]), 0.0)
                lhs_rows.append(jnp.concatenate([cb * lm * dt_t[h:h + 1, :], cm * jnp.exp2(colb)], axis=1))
                colbs.append(colb)
            lhs = jnp.concatenate(lhs_rows, axis=0).astype(BF16)
            xs_p = xs[:, k * LANES:(k + 1) * LANES]
            st_p = st_scr[:, k * LANES:(k + 1) * LANES]
            both = _dot(lhs, jnp.concatenate([xs_p, st_p], axis=0).astype(BF16))
            y_pairs.append(jnp.where(lt64, both[:L], both[L:]))
            wb = jnp.where(lt64, jnp.broadcast_to(wdec[:, 2 * k:2 * k + 1], (L, L)),
                           jnp.broadcast_to(wdec[:, 2 * k + 1:2 * k + 2], (L, L)))
            xw_pairs.append(xs_p * wb)
            dec_pairs.append(jnp.exp2(jnp.where(lt64_row, colbs[0][L - 1:L, :], colbs[1][L - 1:L, :])))
        xw = jnp.concatenate(xw_pairs, axis=1).astype(BF16)
        dec = jnp.concatenate(dec_pairs, axis=1)
        inc = _dot(bm.T.astype(BF16), xw)
        sl = slice(g * GROUP_W, (g + 1) * GROUP_W)
        st_scr[:, sl] = st_scr[:, sl] * dec + inc

    y = jnp.concatenate(y_pairs, axis=1)
    y = y + _mul8(xs, de_ref[...])
    y = y * _silu(z)
    ya = _mul8(_group_rms(y), nw_ref[...]).astype(BF16)

    for p in range(SUBLANES):
        ph[p * HB + SUBLANES:(p + 1) * HB, :] = _blk(pu, p)
    w_max = POOL_WINDOWS[-1]
    lo_max = (len(POOL_WINDOWS) - 1) * POOL_GW
    for p in range(SUBLANES):
        psh1[p * PH:(p + 1) * PH, :] = ph[p * HB + SUBLANES - 1:(p + 1) * HB - 1, :]
        psh2[p * PH:(p + 1) * PH, :] = ph[p * HB + SUBLANES - 2:(p + 1) * HB - 2, lo_max:lo_max + POOL_GW]
    i_col = lax.broadcasted_iota(jnp.int32, (PH, 1), 0)
    d_blocks = []
    for r in range(SUBLANES):
        pos = c * L + i_col * SUBLANES + r
        outs = []
        for gi, w in enumerate(POOL_WINDOWS):
            lo = gi * POOL_GW
            cur = _blk(pu, r)[:, lo:lo + POOL_GW]
            acc = cur
            for k in range(1, w):
                p, shift = (r - k) % SUBLANES, -((r - k) // SUBLANES)
                if shift == 0:
                    src = _blk(pu, p)[:, lo:lo + POOL_GW]
                elif shift == 1:
                    src = psh1[p * PH:(p + 1) * PH, lo:lo + POOL_GW]
                else:
                    assert shift == 2 and w == w_max
                    src = psh2[p * PH:(p + 1) * PH, :]
                acc = acc + src
            outs.append(acc / jnp.minimum(w, pos + 1).astype(F32) - cur)
        d_blocks.append(jnp.concatenate(outs, axis=1))
    d = jnp.concatenate(d_blocks, axis=0).astype(BF16)

    out = _merge(x, ya, d, ga, gb, wsp_ref, pw_ref, lambda v: _mul8(v, ps_ref[...]), wo_ref)
    for r in range(SUBLANES):
        for t in range(n_lt):
            perm[t, pl.ds(r, PH, stride=SUBLANES), :] = _blk(out, r)[:, t * LANES:(t + 1) * LANES]
    for t in range(n_lt):
        o_ref[:, t * LANES:(t + 1) * LANES] = perm[t]

    @pl.when(c == nchunks - 1)
    def _():
        for k in range(D_INNER // LANES):
            sst_ref[0, k * LANES:(k + 1) * LANES, :] = st_scr[:, k * LANES:(k + 1) * LANES].T
        for q in range(len(CONV_WRAP)):
            cst_ref[0, q:q + 1, :] = xh[(q + 1) * HB - 1:(q + 1) * HB, :]
        for n in range(POOL_BUF):
            t = L - POOL_BUF + n
            row = (t % SUBLANES) * HB + SUBLANES + t // SUBLANES
            pst_ref[0, n:n + 1, :] = ph[row:row + 1, :]

    for q in range(len(CONV_WRAP)):
        xh[q * HB:q * HB + SUBLANES, :] = xh[q * HB + PH:(q + 1) * HB, :]
    for p in range(SUBLANES):
        ph[p * HB:p * HB + SUBLANES, :] = ph[p * HB + PH:(p + 1) * HB, :]


def _mixer_prompt(x, W, l, batch, seq):
    nchunks = seq // CHUNK
    tok = lambda b, c: (b * nchunks + c, 0)
    per_batch = lambda *tail: pl.BlockSpec((1,) + tail, lambda b, c: (b,) + (0,) * len(tail))
    return pl.pallas_call(
        functools.partial(_mixer_body, nchunks=nchunks),
        grid=(batch, nchunks),
        in_specs=[pl.BlockSpec((CHUNK, D_MODEL), tok),
                  _layer_spec(l, (SUBLANES, D_MODEL)),
                  _layer_spec(l, (D_MODEL, ZXD_W), resident=True),
                  _layer_spec(l, (D_MODEL, PG_W), resident=True),
                  _layer_spec(l, (CONV_WIDTH, SUBLANES, CONV_DIM)),
                  _layer_spec(l, (SUBLANES, CONV_DIM)),
                  _layer_spec(l, (SUBLANES, LANES)),
                  _layer_spec(l, (SUBLANES, LANES)),
                  _layer_spec(l, (SUBLANES, D_INNER)),
                  _layer_spec(l, (SUBLANES, D_INNER)),
                  _layer_spec(l, (D_INNER, D_MODEL), resident=True),
                  _layer_spec(l, (len(POOL_WINDOWS), POOL_GW, POOL_GW), resident=True),
                  _layer_spec(l, (SUBLANES, D_MODEL)),
                  _layer_spec(l, (D_MODEL, D_MODEL), resident=True)],
        out_specs=[pl.BlockSpec((CHUNK, D_MODEL), tok),
                   per_batch(D_INNER, SSD_STATE),
                   per_batch(CONV_WIDTH - 1, CONV_DIM),
                   per_batch(POOL_BUF, D_MODEL)],
        out_shape=[jax.ShapeDtypeStruct((batch * seq, D_MODEL), F32),
                   jax.ShapeDtypeStruct((batch, D_INNER, SSD_STATE), F32),
                   jax.ShapeDtypeStruct((batch, CONV_WIDTH - 1, CONV_DIM), F32),
                   jax.ShapeDtypeStruct((batch, POOL_BUF, D_MODEL), F32)],
        scratch_shapes=[pltpu.VMEM((D_MODEL // LANES, CHUNK, LANES), F32),
                        pltpu.VMEM((len(CONV_WRAP) * HB, CONV_DIM), F32),
                        pltpu.VMEM((len(CONV_WRAP) * PH, CONV_DIM), F32),
                        pltpu.VMEM((SUBLANES * HB, D_MODEL), F32),
                        pltpu.VMEM((SUBLANES * PH, D_MODEL), F32),
                        pltpu.VMEM((SUBLANES * PH, POOL_GW), F32),
                        pltpu.VMEM((SSD_STATE, D_INNER), F32)],
        compiler_params=_cparams("parallel", "arbitrary"),
        name="mixer_prompt",
    )(x, W["norm_mix8"], W["w_zxd"], W["w_pg"], W["conv_w8"], W["conv_b8"], W["dt_bias8"], W["a_neg8"],
      W["d_e8"], W["ssd_norm8"], W["w_ssd_proj"], W["pool_w"], W["pool_scale8"], W["w_o"])


def _attn_body(*refs, ssm_tb, n_alias):
    x_ref, g_ref, wq_ref, k_ref, v_ref, wo_ref, s_ref, xdt_ref, dec_ref, act_ref = refs[:10]
    o_ref, so_ref, y_ref = refs[10 + n_alias:]
    _attn_core(x_ref, g_ref, wq_ref, k_ref, v_ref, wo_ref, o_ref)
    _ssm_core(s_ref, xdt_ref, dec_ref, act_ref, so_ref, y_ref, ssm_tb)


def _attn_core(x_ref, g_ref, wq_ref, k_ref, v_ref, wo_ref, o_ref):
    x = x_ref[...]
    u = _rms(x, g_ref[...]).astype(BF16)
    q = (_dot(u, wq_ref[...]) * (X_HEAD_DIM ** -0.5)).astype(BF16)
    outs = []
    for h in range(X_HEADS):
        sl = slice(h * X_HEAD_DIM, (h + 1) * X_HEAD_DIM)
        s = _dot_nt(q[:, sl], k_ref[:, sl].astype(BF16))
        p = jnp.exp(s - jnp.max(s, axis=-1, keepdims=True))
        p = p / jnp.sum(p, axis=-1, keepdims=True)
        outs.append(_dot(p.astype(BF16), v_ref[:, sl].astype(BF16)))
    o = jnp.concatenate(outs, axis=1).astype(BF16)
    o_ref[...] = x + _dot(o, wo_ref[...])


def _attn_prompt(x, mk, mv, W, l, batch, seq, states, prev, xdt, dec, act, tm=512):
    tm = min(tm, seq)
    assert seq % tm == 0
    nt = seq // tm
    nsteps = batch * nt
    m = xdt.shape[0]
    assert m % nsteps == 0
    tb = m // nsteps
    tok = lambda b, i: (b * nt + i, 0)
    step3 = lambda b, i: (b * nt + i, 0, 0)
    kv = pl.BlockSpec((None, MEM_LEN, D_MODEL), lambda b, i: (l, b, 0))
    cols = lambda v: v.reshape(nsteps, tb, D_INNER).transpose(0, 2, 1)
    st_spec = pl.BlockSpec((None, tb, D_INNER, SSD_STATE), lambda b, i: (l, b * nt + i, 0, 0))
    col_spec = pl.BlockSpec((1, D_INNER, tb), step3)
    in_specs = [pl.BlockSpec((tm, D_MODEL), tok),
                _layer_spec(l, (1, D_MODEL)),
                _layer_spec(l, (D_MODEL, D_MODEL), resident=True), kv, kv,
                _layer_spec(l, (D_MODEL, D_MODEL), resident=True),
                st_spec, col_spec, col_spec, pl.BlockSpec((1, tb, CONV_DIM), step3)]
    args = [x, W["norm_x"], W["w_xq"], mk, mv, W["w_xo"], states, cols(xdt), cols(dec),
            act.reshape(nsteps, tb, CONV_DIM)]
    aliases = {}
    if prev is not None:
        in_specs.append(pl.BlockSpec(memory_space=pl.ANY))
        args.append(prev)
        aliases = {len(args) - 1: 1}
    x, new_states, y = pl.pallas_call(
        functools.partial(_attn_body, ssm_tb=tb, n_alias=len(aliases)),
        grid=(batch, nt),
        in_specs=in_specs,
        out_specs=[pl.BlockSpec((tm, D_MODEL), tok), st_spec, pl.BlockSpec((1, tb, D_INNER), step3)],
        out_shape=[jax.ShapeDtypeStruct((batch * seq, D_MODEL), F32),
                   jax.ShapeDtypeStruct(states.shape, F32),
                   jax.ShapeDtypeStruct((nsteps, tb, D_INNER), F32)],
        input_output_aliases=aliases,
        compiler_params=_cparams("parallel", "parallel"),
        name="attn_prompt",
    )(*args)
    return x, new_states, y.reshape(m, D_INNER)


def _mlp_body(*refs, att_tb):
    if att_tb:
        x_ref, g_ref, wu_ref, wd_ref, q_ref, k_ref, v_ref, o_ref, so_ref, xm_ref = refs
    else:
        x_ref, g_ref, wu_ref, wd_ref, o_ref, xm_ref = refs

    @pl.when(pl.program_id(1) == 0)
    def _():
        x = x_ref[...]
        xm_ref[...] = _rms(x, g_ref[...]).astype(BF16)
        o_ref[...] = x

    h = jnp.square(jnp.maximum(_dot(xm_ref[...], wu_ref[...]), 0.0)).astype(BF16)
    o_ref[...] += _dot(h, wd_ref[...])
    if att_tb:
        _sattn_core(q_ref, k_ref, v_ref, so_ref, att_tb)


def _mlp(x, W, l, side=None, tm=1024, tf=1024):
    m = x.shape[0]
    tm = min(tm, m)
    assert m % tm == 0
    nk = D_FF // tf
    in_specs = [pl.BlockSpec((tm, D_MODEL), lambda i, k: (i, 0)),
                _layer_spec(l, (1, D_MODEL)),
                pl.BlockSpec((None, D_MODEL, tf), lambda i, k: (l, 0, k)),
                pl.BlockSpec((None, tf, D_MODEL), lambda i, k: (l, k, 0))]
    out_specs = [pl.BlockSpec((tm, D_MODEL), lambda i, k: (i, 0))]
    out_shape = [jax.ShapeDtypeStruct((m, D_MODEL), F32)]
    args = [x, W["norm_mlp"], W["w_up"], W["w_down"]]
    tb = 0
    if side is not None:
        q8, ck, cv = side
        ms = q8.shape[0]
        nsteps = (m // tm) * nk
        assert ms % nsteps == 0
        tb = ms // nsteps
        kv = pl.BlockSpec((None, tb, KV_ROWS, LANES), lambda i, k: (l, i * nk + k, 0, 0))
        qo = pl.BlockSpec((tb, SUBLANES, LANES), lambda i, k: (i * nk + k, 0, 0))
        in_specs += [qo, kv, kv]
        out_specs.append(qo)
        out_shape.append(jax.ShapeDtypeStruct((ms, SUBLANES, LANES), F32))
        args += [q8, ck, cv]
    outs = pl.pallas_call(
        functools.partial(_mlp_body, att_tb=tb),
        grid=(m // tm, nk),
        in_specs=in_specs,
        out_specs=out_specs,
        out_shape=out_shape,
        scratch_shapes=[pltpu.VMEM((tm, D_MODEL), BF16)],
        compiler_params=_cparams("parallel", "arbitrary"),
        name="mlp",
    )(*args)
    return outs[0] if side is None else outs


STEP_TB = 32


def _step_body(x_ref, g_ref, w_ref, wpg_ref, cst_ref, cw_ref, cb_ref, dtb_ref, a_ref, pst_ref,
               ncst_ref, npst_ref, act_ref, xdt_ref, dec_ref, d_ref, z_ref, pg_ref, *, start_pos):
    x = x_ref[...]
    u = _mul8(x * lax.rsqrt(jnp.mean(x * x, axis=-1, keepdims=True) + EPS), g_ref[...]).astype(BF16)
    z_ref[...] = _dot(u, w_ref[:, 0:OFF_XBC])
    xbc = _dot(u, w_ref[:, OFF_XBC:OFF_DT])
    dtr = _dot(u, w_ref[:, OFF_DT:ZXD_W])
    pg = _dot(u, wpg_ref[...])
    pg_ref[...] = pg
    pu = pg[:, 0:D_MODEL]

    conv = None
    for k in range(CONV_WIDTH):
        src = cst_ref[:, k * CONV_DIM:(k + 1) * CONV_DIM] if k < CONV_WIDTH - 1 else xbc
        term = _mul8(src, cw_ref[k])
        conv = _add8(term, cb_ref[...]) if conv is None else conv + term
    for k in range(CONV_WIDTH - 2):
        ncst_ref[:, k * CONV_DIM:(k + 1) * CONV_DIM] = cst_ref[:, (k + 1) * CONV_DIM:(k + 2) * CONV_DIM]
    ncst_ref[:, (CONV_WIDTH - 2) * CONV_DIM:] = xbc
    act = _silu(conv)
    act_ref[...] = act

    dt = _softplus(_add8(dtr, dtb_ref[...]))
    dec = jnp.exp(_mul8(dt, a_ref[...]))
    lt64 = lax.broadcasted_iota(jnp.int32, (1, LANES), 1) < SSD_HEAD_DIM
    m = dt.shape[0]
    for k in range(D_INNER // LANES):
        dt_e = jnp.where(lt64, jnp.broadcast_to(dt[:, 2 * k:2 * k + 1], (m, LANES)),
                         jnp.broadcast_to(dt[:, 2 * k + 1:2 * k + 2], (m, LANES)))
        dec_e = jnp.where(lt64, jnp.broadcast_to(dec[:, 2 * k:2 * k + 1], (m, LANES)),
                          jnp.broadcast_to(dec[:, 2 * k + 1:2 * k + 2], (m, LANES)))
        xdt_ref[:, k * LANES:(k + 1) * LANES] = act[:, k * LANES:(k + 1) * LANES] * dt_e
        dec_ref[:, k * LANES:(k + 1) * LANES] = dec_e

    outs = []
    for gi, w in enumerate(POOL_WINDOWS):
        lo = gi * POOL_GW
        cur = pu[:, lo:lo + POOL_GW]
        acc = cur
        for k in range(1, w):
            j = POOL_BUF - k
            acc = acc + pst_ref[:, j * D_MODEL + lo:j * D_MODEL + lo + POOL_GW]
        outs.append(acc / float(min(w, start_pos + 1)) - cur)
    d_ref[...] = jnp.concatenate(outs, axis=1).astype(d_ref.dtype)
    npst_ref[:, :(POOL_BUF - 1) * D_MODEL] = pst_ref[:, D_MODEL:]
    npst_ref[:, (POOL_BUF - 1) * D_MODEL:] = pu


def _sample_step(x, cst, pst, W, l, start_pos):
    m = x.shape[0]
    tm = min(STEP_TB, m)
    rows = lambda width: pl.BlockSpec((tm, width), lambda i: (i, 0))
    widths = [(CONV_WIDTH - 1) * CONV_DIM, POOL_BUF * D_MODEL, CONV_DIM, D_INNER, D_INNER, D_MODEL, D_INNER, PG_W]
    dtypes = [F32, F32, F32, F32, F32, BF16, F32, F32]
    return pl.pallas_call(
        functools.partial(_step_body, start_pos=start_pos),
        grid=(m // tm,),
        in_specs=[rows(D_MODEL), _layer_spec(l, (SUBLANES, D_MODEL)),
                  _layer_spec(l, (D_MODEL, ZXD_W), resident=True), _layer_spec(l, (D_MODEL, PG_W), resident=True),
                  rows((CONV_WIDTH - 1) * CONV_DIM), _layer_spec(l, (CONV_WIDTH, SUBLANES, CONV_DIM)),
                  _layer_spec(l, (SUBLANES, CONV_DIM)), _layer_spec(l, (SUBLANES, LANES)),
                  _layer_spec(l, (SUBLANES, LANES)), rows(POOL_BUF * D_MODEL)],
        out_specs=[rows(w) for w in widths],
        out_shape=[jax.ShapeDtypeStruct((m, w), dt) for w, dt in zip(widths, dtypes)],
        compiler_params=_cparams("parallel"),
        name="sample_step",
    )(x, W["norm_mix8"], W["w_zxd"], W["w_pg"], cst, W["conv_w8"], W["conv_b8"], W["dt_bias8"], W["a_neg8"], pst)


def _ssm_core(s_ref, xdt_ref, dec_ref, act_ref, so_ref, y_ref, tb):
    for j in range(tb):
        for g in range(SSD_GROUPS):
            rows = slice(g * GROUP_W, (g + 1) * GROUP_W)
            b_row = act_ref[0, j:j + 1, D_INNER + g * SSD_STATE:D_INNER + (g + 1) * SSD_STATE]
            c_row = act_ref[0, j:j + 1, D_INNER + (SSD_GROUPS + g) * SSD_STATE:
                            D_INNER + (SSD_GROUPS + g + 1) * SSD_STATE]
            sn = s_ref[j, rows, :] * dec_ref[0, rows, j:j + 1] + xdt_ref[0, rows, j:j + 1] * b_row
            so_ref[j, rows, :] = sn
            c8 = jnp.broadcast_to(c_row, (SUBLANES, SSD_STATE)).astype(BF16)
            y_ref[0, j:j + 1, rows] = _dot_nt(c8, sn.astype(BF16))[0:1, :]


def _gate_body(y_ref, act_ref, z_ref, de_ref, nw_ref, o_ref):
    y = y_ref[...] + _mul8(act_ref[:, :D_INNER], de_ref[...])
    y = y * _silu(z_ref[...])
    o_ref[...] = _mul8(_group_rms(y), nw_ref[...]).astype(o_ref.dtype)


def _sample_gate(y, act, z, W, l):
    m = y.shape[0]
    full = lambda *shape: pl.BlockSpec(shape, lambda i: (0,) * len(shape))
    return pl.pallas_call(
        _gate_body,
        grid=(1,),
        in_specs=[full(m, D_INNER), full(m, CONV_DIM), full(m, D_INNER),
                  _layer_spec(l, (SUBLANES, D_INNER)), _layer_spec(l, (SUBLANES, D_INNER))],
        out_specs=full(m, D_INNER),
        out_shape=jax.ShapeDtypeStruct((m, D_INNER), BF16),
        compiler_params=_cparams("arbitrary"),
        name="sample_gate",
    )(y, act, z, W["d_e8"], W["ssd_norm8"])


def _mixs_body(ya_ref, d_ref, ga_ref, gb_ref, x_ref, wsp_ref, pw_ref, ps_ref, wo_ref, o_ref):
    o_ref[...] = _merge(x_ref[...], ya_ref[...], d_ref[...], ga_ref[...], gb_ref[...],
                        wsp_ref, pw_ref, lambda v: _mul8(v, ps_ref[...]), wo_ref)


def _mix_sample(ya, d, pg, x, W, l):
    m = x.shape[0]
    blk = lambda width, j: pl.BlockSpec((m, width), lambda i: (0, j))
    return pl.pallas_call(
        _mixs_body,
        grid=(1,),
        in_specs=[blk(D_INNER, 0), blk(D_MODEL, 0), blk(D_MODEL, 1), blk(D_MODEL, 2), blk(D_MODEL, 0),
                  _layer_spec(l, (D_INNER, D_MODEL)), _layer_spec(l, (len(POOL_WINDOWS), POOL_GW, POOL_GW)),
                  _layer_spec(l, (SUBLANES, D_MODEL)), _layer_spec(l, (D_MODEL, D_MODEL))],
        out_specs=blk(D_MODEL, 0),
        out_shape=jax.ShapeDtypeStruct((m, D_MODEL), F32),
        compiler_params=_cparams("arbitrary"),
        name="mix_sample",
    )(ya, d, pg, pg, x, W["w_ssd_proj"], W["pool_w"], W["pool_scale8"], W["w_o"])


KV_ROWS = MEM_LEN * SUBLANES
HALVES = X_HEAD_DIM // LANES
assert HALVES * X_HEADS == SUBLANES


def _sattn_core(q_ref, k_ref, v_ref, o_ref, tb):
    for j in range(tb):
        k3 = k_ref[j].reshape(MEM_LEN, SUBLANES, LANES)
        v3 = v_ref[j].reshape(MEM_LEN, SUBLANES, LANES)
        prod = k3 * q_ref[j][None]
        prod = prod + pltpu.roll(prod, X_HEADS, axis=1)
        s = jnp.sum(prod, axis=-1, keepdims=True)
        e = jnp.exp(s - jnp.max(s, axis=0, keepdims=True))
        pr = e / jnp.sum(e, axis=0, keepdims=True)
        o_ref[j] = jnp.sum(pr * v3, axis=0)


def _kv_rows(c):
    depth, m = c.shape[:2]
    c = c.reshape(depth, m, MEM_LEN, X_HEADS, HALVES, LANES).transpose(0, 1, 2, 4, 3, 5)
    return c.reshape(depth, m, KV_ROWS, LANES)


def _q_rows(q):
    m = q.shape[0]
    return q.reshape(m, X_HEADS, HALVES, LANES).transpose(0, 2, 1, 3).reshape(m, SUBLANES, LANES)


def _o_cols(o8):
    m = o8.shape[0]
    return o8.reshape(m, HALVES, X_HEADS, LANES).transpose(0, 2, 1, 3).reshape(m, D_MODEL)


def _prep_weights(norm_mix, w_in, conv_w, conv_b, dt_bias, a_log, d_skip, ssd_norm, w_ssd_proj, pool_w,
                  pool_scale, w_o, norm_x, w_xq, w_xk, w_xv, w_xo, norm_mlp, w_up, w_down):
    depth = w_in.shape[0]
    lane_pad = lambda v: jnp.concatenate([v, jnp.zeros((depth, LANES - SSD_HEADS), F32)], axis=1)
    rep8 = lambda v: jnp.broadcast_to(v[:, None, :], (depth, SUBLANES, v.shape[-1]))
    row = lambda v: v[:, None, :]
    return dict(
        norm_mix8=rep8(norm_mix),
        w_zxd=w_in[:, :, :ZXD_W].astype(BF16),
        w_pg=w_in[:, :, OFF_POOL:].astype(BF16),
        conv_w8=jnp.broadcast_to(conv_w[:, :, None, :], (depth, CONV_WIDTH, SUBLANES, CONV_DIM)),
        conv_b8=rep8(conv_b),
        dt_bias8=rep8(lane_pad(dt_bias)),
        a_neg8=rep8(lane_pad(-jnp.exp(a_log))),
        d_e8=rep8(jnp.repeat(d_skip, SSD_HEAD_DIM, axis=1)),
        ssd_norm8=rep8(ssd_norm),
        w_ssd_proj=w_ssd_proj.astype(BF16),
        pool_w=pool_w.astype(BF16), pool_scale8=rep8(pool_scale),
        w_o=w_o.astype(BF16), norm_x=row(norm_x), w_xq=w_xq.astype(BF16),
        w_xk=w_xk.astype(BF16), w_xv=w_xv.astype(BF16), w_xo=w_xo.astype(BF16),
        norm_mlp=row(norm_mlp), w_up=w_up.astype(BF16), w_down=w_down.astype(BF16))


def _layer(hp, hs, mk, mv, states, new_states, st_conv, st_pool, ck, cv, W, l, batch, seq):
    m = hs.shape[0]
    ncst, npst, act, xdt, dec, d, z, pg = _sample_step(hs, st_conv.reshape(m, -1), st_pool.reshape(m, -1), W, l,
                                                       PAST_LEN)
    hp, sst, cst, pst = _mixer_prompt(hp, W, l, batch, seq)
    hp, new_states, y = _attn_prompt(hp, mk, mv, W, l, batch, seq, states, new_states, xdt, dec, act)
    ya = _sample_gate(y, act, z, W, l)
    hs = _mix_sample(ya, d, pg, hs, W, l)
    q = _mm(hs, W["w_xq"], l, g=W["norm_x"], out_scale=X_HEAD_DIM ** -0.5)
    hp, o8 = _mlp(hp, W, l, side=(_q_rows(q), ck, cv))
    hs = _mm(_o_cols(o8), W["w_xo"], l, res=hs)
    hs = _mlp(hs, W, l)
    return hp, hs, new_states, (sst, cst, pst, ncst, npst)


def kernel(x_prompt, x_sample, mem_prompt, state_ssm, state_conv, state_pool, cache_mem_k, cache_mem_v,
           norm_mix, w_in, conv_w, conv_b, dt_bias, a_log, d_skip, ssd_norm, w_ssd_proj, pool_w, pool_scale,
           w_o, norm_x, w_xq, w_xk, w_xv, w_xo, norm_mlp, w_up, w_down, norm_final):
    bp, seq, _ = x_prompt.shape
    bs = x_sample.shape[0]
    depth = w_in.shape[0]
    assert x_sample.shape[1] == 1 and seq % CHUNK == 0
    W = _prep_weights(norm_mix, w_in, conv_w, conv_b, dt_bias, a_log, d_skip, ssd_norm, w_ssd_proj, pool_w,
                      pool_scale, w_o, norm_x, w_xq, w_xk, w_xv, w_xo, norm_mlp, w_up, w_down)
    hp = x_prompt.reshape(bp * seq, D_MODEL)
    hs = x_sample.reshape(bs, D_MODEL)
    mem = mem_prompt.reshape(bp * MEM_LEN, D_MODEL)
    mk, mv = _kv_proj(mem, W["w_xk"], W["w_xv"])
    states = state_ssm.reshape(depth, bs, D_INNER, SSD_STATE)
    ck = _kv_rows(cache_mem_k)
    cv = _kv_rows(cache_mem_v)
    new_states = None
    outs = [[] for _ in range(5)]
    for l in range(depth):
        hp, hs, new_states, (sst, cst, pst, ncst, npst) = _layer(
            hp, hs, mk, mv, states, new_states, state_conv[l], state_pool[l], ck, cv, W, l, bp, seq)
        outs[0].append(sst.reshape(bp, SSD_HEADS, SSD_HEAD_DIM, SSD_STATE))
        outs[1].append(cst)
        outs[2].append(pst)
        outs[3].append(ncst.reshape(bs, CONV_WIDTH - 1, CONV_DIM))
        outs[4].append(npst.reshape(bs, POOL_BUF, D_MODEL))
    y_prompt = _rmsnorm(hp, norm_final).reshape(bp, seq, D_MODEL)
    y_sample = _rmsnorm(hs, norm_final).reshape(bs, 1, D_MODEL)
    ssm_p, conv_p, pool_p, conv_s, pool_s = (jnp.stack(o) for o in outs)
    kv_shape = (depth, bp, MEM_LEN, X_HEADS, X_HEAD_DIM)
    return (y_prompt, y_sample, ssm_p, conv_p, pool_p, mk.reshape(kv_shape), mv.reshape(kv_shape),
            new_states.reshape(depth, bs, SSD_HEADS, SSD_HEAD_DIM, SSD_STATE), conv_s, pool_s)
```

```python
import functools
import math

import jax
import jax.numpy as jnp
from jax import lax
from jax.experimental import pallas as pl
from jax.experimental.pallas import tpu as pltpu

F32 = jnp.float32
BF16 = jnp.bfloat16

D_MODEL = 1024
D_INNER = 2048
SSD_HEAD_DIM = 64
SSD_HEADS = 32
SSD_GROUPS = 4
SSD_STATE = 128
GROUP_W = D_INNER // SSD_GROUPS
CONV_WIDTH = 4
CONV_DIM = D_INNER + 2 * SSD_GROUPS * SSD_STATE
CHUNK = 128
POOL_WINDOWS = (2, 4, 8, 16)
POOL_GW = 256
POOL_BUF = 15
MEM_LEN = 256
X_HEADS = 4
X_HEAD_DIM = 256
D_FF = 4096
EPS = 1e-6
PAST_LEN = 16384
LOG2E = math.log2(math.e)
OFF_XBC = D_INNER
OFF_DT = OFF_XBC + CONV_DIM
OFF_POOL = OFF_DT + SSD_HEADS
LANES = 128
SUBLANES = 8
ZXD_W = OFF_DT + LANES
PG_W = 3 * D_MODEL
VMEM_LIMIT = 52 * 1024 * 1024


def _cparams(*sem):
    return pltpu.CompilerParams(dimension_semantics=sem, vmem_limit_bytes=VMEM_LIMIT)


def _layer_spec(l, tail, resident=False):
    mode = dict(pipeline_mode=pl.Buffered(1)) if resident else {}
    return pl.BlockSpec((None,) + tuple(tail), lambda *_: (l,) + (0,) * len(tail), **mode)


def _dot(a, b):
    return jnp.dot(a, b, preferred_element_type=F32)


def _dot_nt(a, b):
    return lax.dot_general(a, b, (((1,), (1,)), ((), ())), preferred_element_type=F32)


def _split3(x):
    hi = x.astype(BF16)
    r1 = x - hi.astype(F32)
    mid = r1.astype(BF16)
    lo = (r1 - mid.astype(F32)).astype(BF16)
    return hi, mid, lo


def _silu(x):
    return x * jax.nn.sigmoid(x)


def _softplus(x):
    return jnp.maximum(x, 0.0) + jnp.log1p(jnp.exp(-jnp.abs(x)))


def _rows8(v, p8, op):
    n = v.shape[-1]
    return op(v.reshape(-1, SUBLANES, n), p8[None]).reshape(v.shape)


def _mul8(v, p8):
    return _rows8(v, p8, jnp.multiply)


def _add8(v, p8):
    return _rows8(v, p8, jnp.add)


def _rms(x, g):
    return x * lax.rsqrt(jnp.mean(x * x, axis=-1, keepdims=True) + EPS) * g


def _group_rms(y):
    parts = []
    for g in range(SSD_GROUPS):
        yg = y[:, g * GROUP_W:(g + 1) * GROUP_W]
        parts.append(yg * lax.rsqrt(jnp.mean(yg * yg, axis=-1, keepdims=True) + EPS))
    return jnp.concatenate(parts, axis=1)


def _merge(ya, d, ga, gb, wsp_ref, pw_ref, ps, wo_ref):
    ssd = _dot(ya, wsp_ref[...])
    pool = jnp.concatenate(
        [_dot(d[:, g * POOL_GW:(g + 1) * POOL_GW], pw_ref[g]) for g in range(len(POOL_WINDOWS))], axis=1)
    merged = jax.nn.sigmoid(ga) * ssd + jax.nn.sigmoid(gb) * ps(pool)
    return _dot(merged.astype(BF16), wo_ref[...])


def _mm_body(*refs, norm, has_res, out_scale):
    it = iter(refs)
    x_ref = next(it)
    w_ref = next(it)
    g_ref = next(it) if norm else None
    r_ref = next(it) if has_res else None
    o_ref = next(it)
    u_ref = next(it)

    @pl.when(pl.program_id(1) == 0)
    def _():
        x = x_ref[...].astype(F32)
        if norm:
            x = _rms(x, g_ref[...])
        u_ref[...] = x.astype(BF16)

    acc = _dot(u_ref[...], w_ref[...])
    if out_scale != 1.0:
        acc = acc * out_scale
    if has_res:
        acc = acc + r_ref[...]
    o_ref[...] = acc.astype(o_ref.dtype)


def _mm(x, w, l, *, g=None, res=None, out_scale=1.0, out_dtype=F32, tm=512, tn=1024):
    m, k = x.shape
    n = w.shape[2]
    tm = min(tm, m)
    tn = min(tn, n)
    assert m % tm == 0 and n % tn == 0, (m, n, tm, tn)
    in_specs = [pl.BlockSpec((tm, k), lambda i, j: (i, 0)),
                pl.BlockSpec((None, k, tn), lambda i, j: (l, 0, j))]
    args = [x, w]
    if g is not None:
        in_specs.append(pl.BlockSpec((None, 1, k), lambda i, j: (l, 0, 0)))
        args.append(g)
    if res is not None:
        in_specs.append(pl.BlockSpec((tm, tn), lambda i, j: (i, j)))
        args.append(res)
    return pl.pallas_call(
        functools.partial(_mm_body, norm=g is not None, has_res=res is not None, out_scale=out_scale),
        grid=(m // tm, n // tn),
        in_specs=in_specs,
        out_specs=pl.BlockSpec((tm, tn), lambda i, j: (i, j)),
        out_shape=jax.ShapeDtypeStruct((m, n), out_dtype),
        scratch_shapes=[pltpu.VMEM((tm, k), BF16)],
        compiler_params=_cparams("parallel", "arbitrary"),
        name="mm",
    )(*args)


def _kv_body(x_ref, wk_ref, wv_ref, k_ref, v_ref, u_ref):
    @pl.when(pl.program_id(1) == 0)
    def _():
        u_ref[...] = x_ref[...].astype(BF16)

    tm = x_ref.shape[0]
    for w_ref, o_ref in ((wk_ref, k_ref), (wv_ref, v_ref)):
        res = _dot(u_ref[...], w_ref[...])
        for h in range(X_HEADS):
            for half in range(HALVES):
                t = h * HALVES + half
                o_ref[pl.ds(half * X_HEADS + h, tm, stride=SUBLANES), :] = res[:, t * LANES:(t + 1) * LANES]


def _kv_proj(mem, wk, wv, tm=512):
    m, k = mem.shape
    depth = wk.shape[0]
    tm = min(tm, m)
    wspec = pl.BlockSpec((None, k, D_MODEL), lambda i, l: (l, 0, 0))
    ospec = pl.BlockSpec((None, tm * SUBLANES, LANES), lambda i, l: (l, i, 0))
    oshape = jax.ShapeDtypeStruct((depth, m * SUBLANES, LANES), F32)
    return pl.pallas_call(
        _kv_body,
        grid=(m // tm, depth),
        in_specs=[pl.BlockSpec((tm, k), lambda i, l: (i, 0)), wspec, wspec],
        out_specs=[ospec, ospec],
        out_shape=[oshape, oshape],
        scratch_shapes=[pltpu.VMEM((tm, k), BF16)],
        compiler_params=_cparams("parallel", "arbitrary"),
        name="kv_proj",
    )(mem, wk, wv)


def _rmsnorm_body(x_ref, g_ref, o_ref):
    o_ref[...] = _rms(x_ref[...], g_ref[...])


def _rmsnorm(x, g, tm=512):
    m, k = x.shape
    tm = min(tm, m)
    return pl.pallas_call(
        _rmsnorm_body,
        grid=(m // tm,),
        in_specs=[pl.BlockSpec((tm, k), lambda i: (i, 0)), pl.BlockSpec((1, k), lambda i: (0, 0))],
        out_specs=pl.BlockSpec((tm, k), lambda i: (i, 0)),
        out_shape=jax.ShapeDtypeStruct((m, k), F32),
        compiler_params=_cparams("parallel"),
        name="rmsnorm",
    )(x, g.reshape(1, k))


PH = CHUNK // SUBLANES
HB = PH + SUBLANES
CONV_WRAP = (5, 6, 7)


def _blk(v, r):
    return v[r * PH:(r + 1) * PH]


def _mixer_body(x_ref, g_ref, wzxd_ref, wpg_ref, cw_ref, cb_ref, dtb_ref, a_ref, de_ref, nw_ref,
                wsp_ref, pw_ref, ps_ref, wo_ref,
                o_ref, sst_ref, cst_ref, pst_ref,
                perm, xh, xsh, ph, psh1, psh2, st_scr, *, nchunks):
    c = pl.program_id(1)
    L = CHUNK

    @pl.when(c == 0)
    def _():
        for q in range(len(CONV_WRAP)):
            xh[q * HB:q * HB + SUBLANES, :] = jnp.zeros((SUBLANES, CONV_DIM), F32)
        for p in range(SUBLANES):
            ph[p * HB:p * HB + SUBLANES, :] = jnp.zeros((SUBLANES, D_MODEL), F32)
        st_scr[...] = jnp.zeros_like(st_scr)

    n_lt = D_MODEL // LANES
    for t in range(n_lt):
        perm[t] = x_ref[:, t * LANES:(t + 1) * LANES]
    x = jnp.concatenate(
        [jnp.concatenate([perm[t, pl.ds(r, PH, stride=SUBLANES), :] for t in range(n_lt)], axis=1)
         for r in range(SUBLANES)], axis=0)
    u = _mul8(x * lax.rsqrt(jnp.mean(x * x, axis=-1, keepdims=True) + EPS), g_ref[...]).astype(BF16)
    xbc = _dot(u, wzxd_ref[:, OFF_XBC:OFF_DT])
    dtr = _dot(u, wzxd_ref[:, OFF_DT:ZXD_W])
    pu = _dot(u, wpg_ref[:, 0:D_MODEL])
    z = _dot(u, wzxd_ref[:, 0:OFF_XBC])
    ga = _dot(u, wpg_ref[:, D_MODEL:2 * D_MODEL])
    gb = _dot(u, wpg_ref[:, 2 * D_MODEL:3 * D_MODEL])

    for q, p in enumerate(CONV_WRAP):
        xh[q * HB + SUBLANES:(q + 1) * HB, :] = _blk(xbc, p)
        xsh[q * PH:(q + 1) * PH, :] = xh[q * HB + SUBLANES - 1:(q + 1) * HB - 1, :]
    conv_blocks = []
    for r in range(SUBLANES):
        acc = None
        for k in range(CONV_WIDTH):
            rp = r - (CONV_WIDTH - 1) + k
            src = _blk(xbc, rp) if rp >= 0 else xsh[(rp + 3) * PH:(rp + 4) * PH, :]
            term = _mul8(src, cw_ref[k])
            acc = _add8(term, cb_ref[...]) if acc is None else acc + term
        conv_blocks.append(acc)
    conv = jnp.concatenate(conv_blocks, axis=0)
    act = _silu(conv)
    xs = act[:, :D_INNER]

    dt = _softplus(_add8(dtr, dtb_ref[...]))
    a = _mul8(dt, a_ref[...])
    j0 = lax.broadcasted_iota(jnp.int32, (L, L), 0)
    j1 = lax.broadcasted_iota(jnp.int32, (L, L), 1)
    tok = lambda j: ((j & (PH - 1)) << 3) | (j >> 4)
    tri = tok(j0) >= tok(j1)
    ltri = jnp.where(tri, 1.0, 0.0).astype(BF16)
    a_hi, a_mid, a_lo = _split3(a)
    cs = (_dot(ltri, a_hi) + _dot(ltri, a_mid) + _dot(ltri, a_lo)) * LOG2E
    cs_t = cs.T
    dt_t = dt.T
    cs_last = cs[L - 1:L, :]
    wdec = dt * jnp.exp2(cs_last - cs)
    lt64 = j1 < SSD_HEAD_DIM
    lt64_row = lax.broadcasted_iota(jnp.int32, (1, L), 1) < SSD_HEAD_DIM

    y_pairs = []
    for g in range(SSD_GROUPS):
        bm = act[:, D_INNER + g * SSD_STATE:D_INNER + (g + 1) * SSD_STATE]
        cm = act[:, D_INNER + (SSD_GROUPS + g) * SSD_STATE:D_INNER + (SSD_GROUPS + g + 1) * SSD_STATE]
        cb = _dot_nt(cm.astype(BF16), bm.astype(BF16))
        xw_pairs = []
        dec_pairs = []
        for kk in range(4):
            k = g * 4 + kk
            lhs_rows, colbs = [], []
            for h in (2 * k, 2 * k + 1):
                colb = jnp.broadcast_to(cs[:, h:h + 1], (L, L))
                lm = jnp.where(tri, jnp.exp2(colb - cs_t[h:h + 1, :]), 0.0)
                lhs_rows.append(jnp.concatenate([cb * lm * dt_t[h:h + 1, :], cm * jnp.exp2(colb)], axis=1))
                colbs.append(colb)
            lhs = jnp.concatenate(lhs_rows, axis=0).astype(BF16)
            xs_p = xs[:, k * LANES:(k + 1) * LANES]
            st_p = st_scr[:, k * LANES:(k + 1) * LANES]
            both = _dot(lhs, jnp.concatenate([xs_p, st_p], axis=0).astype(BF16))
            y_pairs.append(jnp.where(lt64, both[:L], both[L:]))
            wb = jnp.where(lt64, jnp.broadcast_to(wdec[:, 2 * k:2 * k + 1], (L, L)),
                           jnp.broadcast_to(wdec[:, 2 * k + 1:2 * k + 2], (L, L)))
            xw_pairs.append(xs_p * wb)
            dec_pairs.append(jnp.exp2(jnp.where(lt64_row, colbs[0][L - 1:L, :], colbs[1][L - 1:L, :])))
        xw = jnp.concatenate(xw_pairs, axis=1).astype(BF16)
        dec = jnp.concatenate(dec_pairs, axis=1)
        inc = _dot(bm.T.astype(BF16), xw)
        sl = slice(g * GROUP_W, (g + 1) * GROUP_W)
        st_scr[:, sl] = st_scr[:, sl] * dec + inc

    y = jnp.concatenate(y_pairs, axis=1)
    y = y + _mul8(xs, de_ref[...])
    y = y * _silu(z)
    ya = _mul8(_group_rms(y), nw_ref[...]).astype(BF16)

    for p in range(SUBLANES):
        ph[p * HB + SUBLANES:(p + 1) * HB, :] = _blk(pu, p)
    w_max = POOL_WINDOWS[-1]
    lo_max = (len(POOL_WINDOWS) - 1) * POOL_GW
    for p in range(SUBLANES):
        psh1[p * PH:(p + 1) * PH, :] = ph[p * HB + SUBLANES - 1:(p + 1) * HB - 1, :]
        psh2[p * PH:(p + 1) * PH, :] = ph[p * HB + SUBLANES - 2:(p + 1) * HB - 2, lo_max:lo_max + POOL_GW]
    i_col = lax.broadcasted_iota(jnp.int32, (PH, 1), 0)
    d_blocks = []
    for r in range(SUBLANES):
        pos = c * L + i_col * SUBLANES + r
        outs = []
        for gi, w in enumerate(POOL_WINDOWS):
            lo = gi * POOL_GW
            cur = _blk(pu, r)[:, lo:lo + POOL_GW]
            acc = cur
            for k in range(1, w):
                p, shift = (r - k) % SUBLANES, -((r - k) // SUBLANES)
                if shift == 0:
                    src = _blk(pu, p)[:, lo:lo + POOL_GW]
                elif shift == 1:
                    src = psh1[p * PH:(p + 1) * PH, lo:lo + POOL_GW]
                else:
                    assert shift == 2 and w == w_max
                    src = psh2[p * PH:(p + 1) * PH, :]
                acc = acc + src
            outs.append(acc / jnp.minimum(w, pos + 1).astype(F32) - cur)
        d_blocks.append(jnp.concatenate(outs, axis=1))
    d = jnp.concatenate(d_blocks, axis=0).astype(BF16)

    out = x + _merge(ya, d, ga, gb, wsp_ref, pw_ref, lambda v: _mul8(v, ps_ref[...]), wo_ref)
    for r in range(SUBLANES):
        for t in range(n_lt):
            perm[t, pl.ds(r, PH, stride=SUBLANES), :] = _blk(out, r)[:, t * LANES:(t + 1) * LANES]
    for t in range(n_lt):
        o_ref[:, t * LANES:(t + 1) * LANES] = perm[t]

    @pl.when(c == nchunks - 1)
    def _():
        for k in range(D_INNER // LANES):
            sst_ref[0, k * LANES:(k + 1) * LANES, :] = st_scr[:, k * LANES:(k + 1) * LANES].T
        for q in range(len(CONV_WRAP)):
            cst_ref[0, q:q + 1, :] = xh[(q + 1) * HB - 1:(q + 1) * HB, :]
        for n in range(POOL_BUF):
            t = L - POOL_BUF + n
            row = (t % SUBLANES) * HB + SUBLANES + t // SUBLANES
            pst_ref[0, n:n + 1, :] = ph[row:row + 1, :]

    for q in range(len(CONV_WRAP)):
        xh[q * HB:q * HB + SUBLANES, :] = xh[q * HB + PH:(q + 1) * HB, :]
    for p in range(SUBLANES):
        ph[p * HB:p * HB + SUBLANES, :] = ph[p * HB + PH:(p + 1) * HB, :]


def _mixer_prompt(x, W, l, batch, seq):
    nchunks = seq // CHUNK
    tok = lambda b, c: (b * nchunks + c, 0)
    per_batch = lambda *tail: pl.BlockSpec((1,) + tail, lambda b, c: (b,) + (0,) * len(tail))
    return pl.pallas_call(
        functools.partial(_mixer_body, nchunks=nchunks),
        grid=(batch, nchunks),
        in_specs=[pl.BlockSpec((CHUNK, D_MODEL), tok),
                  _layer_spec(l, (SUBLANES, D_MODEL)),
                  _layer_spec(l, (D_MODEL, ZXD_W), resident=True),
                  _layer_spec(l, (D_MODEL, PG_W), resident=True),
                  _layer_spec(l, (CONV_WIDTH, SUBLANES, CONV_DIM)),
                  _layer_spec(l, (SUBLANES, CONV_DIM)),
                  _layer_spec(l, (SUBLANES, LANES)),
                  _layer_spec(l, (SUBLANES, LANES)),
                  _layer_spec(l, (SUBLANES, D_INNER)),
                  _layer_spec(l, (SUBLANES, D_INNER)),
                  _layer_spec(l, (D_INNER, D_MODEL), resident=True),
                  _layer_spec(l, (len(POOL_WINDOWS), POOL_GW, POOL_GW), resident=True),
                  _layer_spec(l, (SUBLANES, D_MODEL)),
                  _layer_spec(l, (D_MODEL, D_MODEL), resident=True)],
        out_specs=[pl.BlockSpec((CHUNK, D_MODEL), tok),
                   per_batch(D_INNER, SSD_STATE),
                   per_batch(CONV_WIDTH - 1, CONV_DIM),
                   per_batch(POOL_BUF, D_MODEL)],
        out_shape=[jax.ShapeDtypeStruct((batch * seq, D_MODEL), F32),
                   jax.ShapeDtypeStruct((batch, D_INNER, SSD_STATE), F32),
                   jax.ShapeDtypeStruct((batch, CONV_WIDTH - 1, CONV_DIM), F32),
                   jax.ShapeDtypeStruct((batch, POOL_BUF, D_MODEL), F32)],
        scratch_shapes=[pltpu.VMEM((D_MODEL // LANES, CHUNK, LANES), F32),
                        pltpu.VMEM((len(CONV_WRAP) * HB, CONV_DIM), F32),
                        pltpu.VMEM((len(CONV_WRAP) * PH, CONV_DIM), F32),
                        pltpu.VMEM((SUBLANES * HB, D_MODEL), F32),
                        pltpu.VMEM((SUBLANES * PH, D_MODEL), F32),
                        pltpu.VMEM((SUBLANES * PH, POOL_GW), F32),
                        pltpu.VMEM((SSD_STATE, D_INNER), F32)],
        compiler_params=_cparams("parallel", "arbitrary"),
        name="mixer_prompt",
    )(x, W["norm_mix8"], W["w_zxd"], W["w_pg"], W["conv_w8"], W["conv_b8"], W["dt_bias8"], W["a_neg8"],
      W["d_e8"], W["ssd_norm8"], W["w_ssd_proj"], W["pool_w"], W["pool_scale8"], W["w_o"])


def _attn_body(*refs, ssm_tb, n_alias):
    x_ref, g_ref, wq_ref, k_ref, v_ref, wo_ref, s_ref, xdt_ref, dec_ref, act_ref = refs[:10]
    o_ref, so_ref, y_ref = refs[10 + n_alias:]
    _attn_core(x_ref, g_ref, wq_ref, k_ref, v_ref, wo_ref, o_ref)
    _ssm_core(s_ref, xdt_ref, dec_ref, act_ref, so_ref, y_ref, ssm_tb)


def _attn_core(x_ref, g_ref, wq_ref, k_ref, v_ref, wo_ref, o_ref):
    x = x_ref[...]
    u = _rms(x, g_ref[...]).astype(BF16)
    q = (_dot(u, wq_ref[...]) * (X_HEAD_DIM ** -0.5)).astype(BF16)
    outs = []
    head = lambda ref, h: jnp.concatenate(
        [ref[pl.ds(half * X_HEADS + h, MEM_LEN, stride=SUBLANES), :] for half in range(HALVES)],
        axis=1).astype(BF16)
    for h in range(X_HEADS):
        s = _dot_nt(q[:, h * X_HEAD_DIM:(h + 1) * X_HEAD_DIM], head(k_ref, h))
        p = jnp.exp(s - jnp.max(s, axis=-1, keepdims=True))
        p = p / jnp.sum(p, axis=-1, keepdims=True)
        outs.append(_dot(p.astype(BF16), head(v_ref, h)))
    o = jnp.concatenate(outs, axis=1).astype(BF16)
    o_ref[...] = x + _dot(o, wo_ref[...])


def _attn_prompt(x, mk, mv, W, l, batch, seq, states, prev, xdt, dec, act, tm=512):
    tm = min(tm, seq)
    assert seq % tm == 0
    nt = seq // tm
    nsteps = batch * nt
    m = xdt.shape[0]
    assert m % nsteps == 0
    tb = m // nsteps
    tok = lambda b, i: (b * nt + i, 0)
    step3 = lambda b, i: (b * nt + i, 0, 0)
    kv = pl.BlockSpec((None, KV_ROWS, LANES), lambda b, i: (l, b, 0))
    cols = lambda v: v.reshape(nsteps, tb, D_INNER).transpose(0, 2, 1)
    st_spec = pl.BlockSpec((None, tb, D_INNER, SSD_STATE), lambda b, i: (l, b * nt + i, 0, 0))
    col_spec = pl.BlockSpec((1, D_INNER, tb), step3)
    in_specs = [pl.BlockSpec((tm, D_MODEL), tok),
                _layer_spec(l, (1, D_MODEL)),
                _layer_spec(l, (D_MODEL, D_MODEL), resident=True), kv, kv,
                _layer_spec(l, (D_MODEL, D_MODEL), resident=True),
                st_spec, col_spec, col_spec, pl.BlockSpec((1, tb, CONV_DIM), step3)]
    args = [x, W["norm_x"], W["w_xq"], mk, mv, W["w_xo"], states, cols(xdt), cols(dec),
            act.reshape(nsteps, tb, CONV_DIM)]
    aliases = {}
    if prev is not None:
        in_specs.append(pl.BlockSpec(memory_space=pl.ANY))
        args.append(prev)
        aliases = {len(args) - 1: 1}
    x, new_states, y = pl.pallas_call(
        functools.partial(_attn_body, ssm_tb=tb, n_alias=len(aliases)),
        grid=(batch, nt),
        in_specs=in_specs,
        out_specs=[pl.BlockSpec((tm, D_MODEL), tok), st_spec, pl.BlockSpec((1, tb, D_INNER), step3)],
        out_shape=[jax.ShapeDtypeStruct((batch * seq, D_MODEL), F32),
                   jax.ShapeDtypeStruct(states.shape, F32),
                   jax.ShapeDtypeStruct((nsteps, tb, D_INNER), F32)],
        input_output_aliases=aliases,
        compiler_params=_cparams("parallel", "parallel"),
        name="attn_prompt",
    )(*args)
    return x, new_states, y.reshape(m, D_INNER)


def _mlp_body(*refs, att_tb):
    if att_tb:
        x_ref, g_ref, wu_ref, wd_ref, q_ref, k_ref, v_ref, o_ref, so_ref, xm_ref = refs
    else:
        x_ref, g_ref, wu_ref, wd_ref, o_ref, xm_ref = refs

    @pl.when(pl.program_id(1) == 0)
    def _():
        x = x_ref[...]
        xm_ref[...] = _rms(x, g_ref[...]).astype(BF16)
        o_ref[...] = x

    h = jnp.square(jnp.maximum(_dot(xm_ref[...], wu_ref[...]), 0.0)).astype(BF16)
    o_ref[...] += _dot(h, wd_ref[...])
    if att_tb:
        _sattn_core(q_ref, k_ref, v_ref, so_ref, att_tb)


def _mlp(x, W, l, side=None, tm=1024, tf=1024):
    m = x.shape[0]
    tm = min(tm, m)
    assert m % tm == 0
    nk = D_FF // tf
    in_specs = [pl.BlockSpec((tm, D_MODEL), lambda i, k: (i, 0)),
                _layer_spec(l, (1, D_MODEL)),
                pl.BlockSpec((None, D_MODEL, tf), lambda i, k: (l, 0, k)),
                pl.BlockSpec((None, tf, D_MODEL), lambda i, k: (l, k, 0))]
    out_specs = [pl.BlockSpec((tm, D_MODEL), lambda i, k: (i, 0))]
    out_shape = [jax.ShapeDtypeStruct((m, D_MODEL), F32)]
    args = [x, W["norm_mlp"], W["w_up"], W["w_down"]]
    tb = 0
    if side is not None:
        q8, ck, cv = side
        ms = q8.shape[0]
        nsteps = (m // tm) * nk
        assert ms % nsteps == 0
        tb = ms // nsteps
        kv = pl.BlockSpec((None, tb, KV_ROWS, LANES), lambda i, k: (l, i * nk + k, 0, 0))
        qo = pl.BlockSpec((tb, SUBLANES, LANES), lambda i, k: (i * nk + k, 0, 0))
        in_specs += [qo, kv, kv]
        out_specs.append(qo)
        out_shape.append(jax.ShapeDtypeStruct((ms, SUBLANES, LANES), F32))
        args += [q8, ck, cv]
    outs = pl.pallas_call(
        functools.partial(_mlp_body, att_tb=tb),
        grid=(m // tm, nk),
        in_specs=in_specs,
        out_specs=out_specs,
        out_shape=out_shape,
        scratch_shapes=[pltpu.VMEM((tm, D_MODEL), BF16)],
        compiler_params=_cparams("parallel", "arbitrary"),
        name="mlp",
    )(*args)
    return outs[0] if side is None else outs


STEP_TB = 32


def _step_body(x_ref, g_ref, w_ref, wpg_ref, cst_ref, cw_ref, cb_ref, dtb_ref, a_ref, pst_ref, *rest, start_pos):
    ncst_ref, npst_ref, act_ref, xdt_ref, dec_ref, d_ref, z_ref, pg_ref = rest[-8:]
    x = x_ref[...]
    u = _mul8(x * lax.rsqrt(jnp.mean(x * x, axis=-1, keepdims=True) + EPS), g_ref[...]).astype(BF16)
    z_ref[...] = _dot(u, w_ref[:, 0:OFF_XBC])
    xbc = _dot(u, w_ref[:, OFF_XBC:OFF_DT])
    dtr = _dot(u, w_ref[:, OFF_DT:ZXD_W])
    pg = _dot(u, wpg_ref[...])
    pg_ref[...] = pg
    pu = pg[:, 0:D_MODEL]

    conv = None
    for k in range(CONV_WIDTH):
        src = cst_ref[k] if k < CONV_WIDTH - 1 else xbc
        term = _mul8(src, cw_ref[k])
        conv = _add8(term, cb_ref[...]) if conv is None else conv + term
    for k in range(CONV_WIDTH - 2):
        ncst_ref[k] = cst_ref[k + 1]
    ncst_ref[CONV_WIDTH - 2] = xbc
    act = _silu(conv)
    act_ref[...] = act

    dt = _softplus(_add8(dtr, dtb_ref[...]))
    dec = jnp.exp(_mul8(dt, a_ref[...]))
    lt64 = lax.broadcasted_iota(jnp.int32, (1, LANES), 1) < SSD_HEAD_DIM
    m = dt.shape[0]
    for k in range(D_INNER // LANES):
        dt_e = jnp.where(lt64, jnp.broadcast_to(dt[:, 2 * k:2 * k + 1], (m, LANES)),
                         jnp.broadcast_to(dt[:, 2 * k + 1:2 * k + 2], (m, LANES)))
        dec_e = jnp.where(lt64, jnp.broadcast_to(dec[:, 2 * k:2 * k + 1], (m, LANES)),
                          jnp.broadcast_to(dec[:, 2 * k + 1:2 * k + 2], (m, LANES)))
        xdt_ref[:, k * LANES:(k + 1) * LANES] = act[:, k * LANES:(k + 1) * LANES] * dt_e
        dec_ref[:, k * LANES:(k + 1) * LANES] = dec_e

    outs = []
    for gi, w in enumerate(POOL_WINDOWS):
        lo = gi * POOL_GW
        cur = pu[:, lo:lo + POOL_GW]
        acc = cur
        for k in range(1, w):
            acc = acc + pst_ref[POOL_BUF - k, :, lo:lo + POOL_GW]
        outs.append(acc / float(min(w, start_pos + 1)) - cur)
    d_ref[...] = jnp.concatenate(outs, axis=1).astype(d_ref.dtype)
    for j in range(POOL_BUF - 1):
        npst_ref[j] = pst_ref[j + 1]
    npst_ref[POOL_BUF - 1] = pu


def _sample_step(x, cst_t, pst_t, prev, W, l, start_pos):
    m = x.shape[0]
    tm = min(STEP_TB, m)
    rows = lambda width: pl.BlockSpec((tm, width), lambda i: (i, 0))
    state = lambda r, w: pl.BlockSpec((None, r, tm, w), lambda i: (l, 0, i, 0))
    widths = [CONV_DIM, D_INNER, D_INNER, D_MODEL, D_INNER, PG_W]
    dtypes = [F32, F32, F32, BF16, F32, F32]
    in_specs = [rows(D_MODEL), _layer_spec(l, (SUBLANES, D_MODEL)),
                _layer_spec(l, (D_MODEL, ZXD_W), resident=True), _layer_spec(l, (D_MODEL, PG_W), resident=True),
                state(CONV_WIDTH - 1, CONV_DIM), _layer_spec(l, (CONV_WIDTH, SUBLANES, CONV_DIM)),
                _layer_spec(l, (SUBLANES, CONV_DIM)), _layer_spec(l, (SUBLANES, LANES)),
                _layer_spec(l, (SUBLANES, LANES)), state(POOL_BUF, D_MODEL)]
    args = [x, W["norm_mix8"], W["w_zxd"], W["w_pg"], cst_t, W["conv_w8"], W["conv_b8"], W["dt_bias8"],
            W["a_neg8"], pst_t]
    aliases = {}
    if prev is not None:
        aliases = {len(args): 0, len(args) + 1: 1}
        in_specs += [pl.BlockSpec(memory_space=pl.ANY)] * 2
        args += list(prev)
    return pl.pallas_call(
        functools.partial(_step_body, start_pos=start_pos),
        grid=(m // tm,),
        in_specs=in_specs,
        out_specs=[state(CONV_WIDTH - 1, CONV_DIM), state(POOL_BUF, D_MODEL)] + [rows(w) for w in widths],
        out_shape=[jax.ShapeDtypeStruct(cst_t.shape, F32), jax.ShapeDtypeStruct(pst_t.shape, F32)]
        + [jax.ShapeDtypeStruct((m, w), dt) for w, dt in zip(widths, dtypes)],
        input_output_aliases=aliases,
        compiler_params=_cparams("parallel"),
        name="sample_step",
    )(*args)


def _ssm_core(s_ref, xdt_ref, dec_ref, act_ref, so_ref, y_ref, tb):
    for j in range(tb):
        for g in range(SSD_GROUPS):
            rows = slice(g * GROUP_W, (g + 1) * GROUP_W)
            b_row = act_ref[0, j:j + 1, D_INNER + g * SSD_STATE:D_INNER + (g + 1) * SSD_STATE]
            c_row = act_ref[0, j:j + 1, D_INNER + (SSD_GROUPS + g) * SSD_STATE:
                            D_INNER + (SSD_GROUPS + g + 1) * SSD_STATE]
            sn = s_ref[j, rows, :] * dec_ref[0, rows, j:j + 1] + xdt_ref[0, rows, j:j + 1] * b_row
            so_ref[j, rows, :] = sn
            c8 = jnp.broadcast_to(c_row, (SUBLANES, SSD_STATE)).astype(BF16)
            y_ref[0, j:j + 1, rows] = _dot_nt(c8, sn.astype(BF16))[0:1, :]


def _gate_body(y_ref, act_ref, z_ref, de_ref, nw_ref, o_ref):
    y = y_ref[...] + _mul8(act_ref[:, :D_INNER], de_ref[...])
    y = y * _silu(z_ref[...])
    o_ref[...] = _mul8(_group_rms(y), nw_ref[...]).astype(o_ref.dtype)


def _sample_gate(y, act, z, W, l):
    m = y.shape[0]
    full = lambda *shape: pl.BlockSpec(shape, lambda i: (0,) * len(shape))
    return pl.pallas_call(
        _gate_body,
        grid=(1,),
        in_specs=[full(m, D_INNER), full(m, CONV_DIM), full(m, D_INNER),
                  _layer_spec(l, (SUBLANES, D_INNER)), _layer_spec(l, (SUBLANES, D_INNER))],
        out_specs=full(m, D_INNER),
        out_shape=jax.ShapeDtypeStruct((m, D_INNER), BF16),
        compiler_params=_cparams("arbitrary"),
        name="sample_gate",
    )(y, act, z, W["d_e8"], W["ssd_norm8"])


def _mixs_body(ya_ref, d_ref, ga_ref, gb_ref, x_ref, wsp_ref, pw_ref, ps_ref, wo_ref, o_ref):
    o_ref[...] = x_ref[...] + _merge(ya_ref[...], d_ref[...], ga_ref[...], gb_ref[...],
                                     wsp_ref, pw_ref, lambda v: _mul8(v, ps_ref[...]), wo_ref)


def _mix_sample(ya, d, pg, x, W, l):
    m = x.shape[0]
    blk = lambda width, j: pl.BlockSpec((m, width), lambda i: (0, j))
    return pl.pallas_call(
        _mixs_body,
        grid=(1,),
        in_specs=[blk(D_INNER, 0), blk(D_MODEL, 0), blk(D_MODEL, 1), blk(D_MODEL, 2), blk(D_MODEL, 0),
                  _layer_spec(l, (D_INNER, D_MODEL)), _layer_spec(l, (len(POOL_WINDOWS), POOL_GW, POOL_GW)),
                  _layer_spec(l, (SUBLANES, D_MODEL)), _layer_spec(l, (D_MODEL, D_MODEL))],
        out_specs=blk(D_MODEL, 0),
        out_shape=jax.ShapeDtypeStruct((m, D_MODEL), F32),
        compiler_params=_cparams("arbitrary"),
        name="mix_sample",
    )(ya, d, pg, pg, x, W["w_ssd_proj"], W["pool_w"], W["pool_scale8"], W["w_o"])


KV_ROWS = MEM_LEN * SUBLANES
HALVES = X_HEAD_DIM // LANES
assert HALVES * X_HEADS == SUBLANES


def _sattn_core(q_ref, k_ref, v_ref, o_ref, tb):
    for j in range(tb):
        k3 = k_ref[j].reshape(MEM_LEN, SUBLANES, LANES)
        v3 = v_ref[j].reshape(MEM_LEN, SUBLANES, LANES)
        prod = k3 * q_ref[j][None]
        prod = prod + pltpu.roll(prod, X_HEADS, axis=1)
        s = jnp.sum(prod, axis=-1, keepdims=True)
        e = jnp.exp(s - jnp.max(s, axis=0, keepdims=True))
        pr = e / jnp.sum(e, axis=0, keepdims=True)
        o_ref[j] = jnp.sum(pr * v3, axis=0)


def _kv_rows(c):
    depth, m = c.shape[:2]
    c = c.reshape(depth, m, MEM_LEN, X_HEADS, HALVES, LANES).transpose(0, 1, 2, 4, 3, 5)
    return c.reshape(depth, m, KV_ROWS, LANES)


def _kv_unrows(r, batch):
    depth = r.shape[0]
    r = r.reshape(depth, batch, MEM_LEN, HALVES, X_HEADS, LANES).transpose(0, 1, 2, 4, 3, 5)
    return r.reshape(depth, batch, MEM_LEN, X_HEADS, X_HEAD_DIM)


def _q_rows(q):
    m = q.shape[0]
    return q.reshape(m, X_HEADS, HALVES, LANES).transpose(0, 2, 1, 3).reshape(m, SUBLANES, LANES)


def _o_cols(o8):
    m = o8.shape[0]
    return o8.reshape(m, HALVES, X_HEADS, LANES).transpose(0, 2, 1, 3).reshape(m, D_MODEL)


def _prep_weights(norm_mix, w_in, conv_w, conv_b, dt_bias, a_log, d_skip, ssd_norm, w_ssd_proj, pool_w,
                  pool_scale, w_o, norm_x, w_xq, w_xk, w_xv, w_xo, norm_mlp, w_up, w_down):
    depth = w_in.shape[0]
    lane_pad = lambda v: jnp.concatenate([v, jnp.zeros((depth, LANES - SSD_HEADS), F32)], axis=1)
    rep8 = lambda v: jnp.broadcast_to(v[:, None, :], (depth, SUBLANES, v.shape[-1]))
    row = lambda v: v[:, None, :]
    return dict(
        norm_mix8=rep8(norm_mix),
        w_zxd=w_in[:, :, :ZXD_W].astype(BF16),
        w_pg=w_in[:, :, OFF_POOL:].astype(BF16),
        conv_w8=jnp.broadcast_to(conv_w[:, :, None, :], (depth, CONV_WIDTH, SUBLANES, CONV_DIM)),
        conv_b8=rep8(conv_b),
        dt_bias8=rep8(lane_pad(dt_bias)),
        a_neg8=rep8(lane_pad(-jnp.exp(a_log))),
        d_e8=rep8(jnp.repeat(d_skip, SSD_HEAD_DIM, axis=1)),
        ssd_norm8=rep8(ssd_norm),
        w_ssd_proj=w_ssd_proj.astype(BF16),
        pool_w=pool_w.astype(BF16), pool_scale8=rep8(pool_scale),
        w_o=w_o.astype(BF16), norm_x=row(norm_x), w_xq=w_xq.astype(BF16),
        w_xk=w_xk.astype(BF16), w_xv=w_xv.astype(BF16), w_xo=w_xo.astype(BF16),
        norm_mlp=row(norm_mlp), w_up=w_up.astype(BF16), w_down=w_down.astype(BF16))


def _layer(hp, hs, mk, mv, states, new_states, cst_t, pst_t, new_hist, ck, cv, W, l, batch, seq):
    ncst, npst, act, xdt, dec, d, z, pg = _sample_step(hs, cst_t, pst_t, new_hist, W, l, PAST_LEN)
    hp, sst, cst, pst = _mixer_prompt(hp, W, l, batch, seq)
    hp, new_states, y = _attn_prompt(hp, mk, mv, W, l, batch, seq, states, new_states, xdt, dec, act)
    ya = _sample_gate(y, act, z, W, l)
    hs = _mix_sample(ya, d, pg, hs, W, l)
    q = _mm(hs, W["w_xq"], l, g=W["norm_x"], out_scale=X_HEAD_DIM ** -0.5)
    hp, o8 = _mlp(hp, W, l, side=(_q_rows(q), ck, cv))
    hs = _mm(_o_cols(o8), W["w_xo"], l, res=hs)
    hs = _mlp(hs, W, l)
    return hp, hs, new_states, (ncst, npst), (sst, cst, pst)


def kernel(x_prompt, x_sample, mem_prompt, state_ssm, state_conv, state_pool, cache_mem_k, cache_mem_v,
           norm_mix, w_in, conv_w, conv_b, dt_bias, a_log, d_skip, ssd_norm, w_ssd_proj, pool_w, pool_scale,
           w_o, norm_x, w_xq, w_xk, w_xv, w_xo, norm_mlp, w_up, w_down, norm_final):
    bp, seq, _ = x_prompt.shape
    bs = x_sample.shape[0]
    depth = w_in.shape[0]
    assert x_sample.shape[1] == 1 and seq % CHUNK == 0
    W = _prep_weights(norm_mix, w_in, conv_w, conv_b, dt_bias, a_log, d_skip, ssd_norm, w_ssd_proj, pool_w,
                      pool_scale, w_o, norm_x, w_xq, w_xk, w_xv, w_xo, norm_mlp, w_up, w_down)
    hp = x_prompt.reshape(bp * seq, D_MODEL)
    hs = x_sample.reshape(bs, D_MODEL)
    mem = mem_prompt.reshape(bp * MEM_LEN, D_MODEL)
    mk, mv = _kv_proj(mem, W["w_xk"], W["w_xv"])
    states = state_ssm.reshape(depth, bs, D_INNER, SSD_STATE)
    ck = _kv_rows(cache_mem_k)
    cv = _kv_rows(cache_mem_v)
    by_row = lambda v: v.transpose(0, 2, 1, 3)
    cst_t, pst_t = by_row(state_conv), by_row(state_pool)
    new_states = new_hist = None
    outs = [[] for _ in range(3)]
    for l in range(depth):
        hp, hs, new_states, new_hist, (sst, cst, pst) = _layer(
            hp, hs, mk, mv, states, new_states, cst_t, pst_t, new_hist, ck, cv, W, l, bp, seq)
        outs[0].append(sst.reshape(bp, SSD_HEADS, SSD_HEAD_DIM, SSD_STATE))
        outs[1].append(cst)
        outs[2].append(pst)
    y_prompt = _rmsnorm(hp, norm_final).reshape(bp, seq, D_MODEL)
    y_sample = _rmsnorm(hs, norm_final).reshape(bs, 1, D_MODEL)
    ssm_p, conv_p, pool_p = (jnp.stack(o) for o in outs)
    return (y_prompt, y_sample, ssm_p, conv_p, pool_p, _kv_unrows(mk, bp), _kv_unrows(mv, bp),
            new_states.reshape(depth, bs, SSD_HEADS, SSD_HEAD_DIM, SSD_STATE),
            by_row(new_hist[0]), by_row(new_hist[1]))
```

```python
import functools
import math

import jax
import jax.numpy as jnp
from jax import lax
from jax.experimental import pallas as pl
from jax.experimental.pallas import tpu as pltpu

F32 = jnp.float32
BF16 = jnp.bfloat16

D_MODEL = 1024
D_INNER = 2048
SSD_HEAD_DIM = 64
SSD_HEADS = 32
SSD_GROUPS = 4
SSD_STATE = 128
GROUP_W = D_INNER // SSD_GROUPS
CONV_WIDTH = 4
CONV_DIM = D_INNER + 2 * SSD_GROUPS * SSD_STATE
CHUNK = 128
POOL_WINDOWS = (2, 4, 8, 16)
POOL_GW = 256
POOL_BUF = 15
MEM_LEN = 256
X_HEADS = 4
X_HEAD_DIM = 256
D_FF = 4096
EPS = 1e-6
PAST_LEN = 16384
LOG2E = math.log2(math.e)
OFF_XBC = D_INNER
OFF_DT = OFF_XBC + CONV_DIM
OFF_POOL = OFF_DT + SSD_HEADS
LANES = 128
SUBLANES = 8
ZXD_W = OFF_DT + LANES
PG_W = 3 * D_MODEL
VMEM_LIMIT = 52 * 1024 * 1024


def _cparams(*sem):
    return pltpu.CompilerParams(dimension_semantics=sem, vmem_limit_bytes=VMEM_LIMIT)


def _layer_spec(l, tail, resident=False):
    mode = dict(pipeline_mode=pl.Buffered(1)) if resident else {}
    return pl.BlockSpec((None,) + tuple(tail), lambda *_: (l,) + (0,) * len(tail), **mode)


def _dot(a, b):
    return jnp.dot(a, b, preferred_element_type=F32)


def _dot_nt(a, b):
    return lax.dot_general(a, b, (((1,), (1,)), ((), ())), preferred_element_type=F32)


def _split3(x):
    hi = x.astype(BF16)
    r1 = x - hi.astype(F32)
    mid = r1.astype(BF16)
    lo = (r1 - mid.astype(F32)).astype(BF16)
    return hi, mid, lo


def _silu(x):
    return x * jax.nn.sigmoid(x)


def _softplus(x):
    return jnp.maximum(x, 0.0) + jnp.log1p(jnp.exp(-jnp.abs(x)))


def _rows8(v, p8, op):
    n = v.shape[-1]
    return op(v.reshape(-1, SUBLANES, n), p8[None]).reshape(v.shape)


def _mul8(v, p8):
    return _rows8(v, p8, jnp.multiply)


def _add8(v, p8):
    return _rows8(v, p8, jnp.add)


def _rms(x, g):
    return x * lax.rsqrt(jnp.mean(x * x, axis=-1, keepdims=True) + EPS) * g


def _group_rms(y):
    parts = []
    for g in range(SSD_GROUPS):
        yg = y[:, g * GROUP_W:(g + 1) * GROUP_W]
        parts.append(yg * lax.rsqrt(jnp.mean(yg * yg, axis=-1, keepdims=True) + EPS))
    return jnp.concatenate(parts, axis=1)


def _merge(ya, d, ga, gb, wsp_ref, pw_ref, ps, wo_ref):
    ssd = _dot(ya, wsp_ref[...])
    pool = jnp.concatenate(
        [_dot(d[:, g * POOL_GW:(g + 1) * POOL_GW], pw_ref[g]) for g in range(len(POOL_WINDOWS))], axis=1)
    merged = jax.nn.sigmoid(ga) * ssd + jax.nn.sigmoid(gb) * ps(pool)
    return _dot(merged.astype(BF16), wo_ref[...])


def _mm_body(*refs, norm, has_res, out_scale):
    it = iter(refs)
    x_ref = next(it)
    w_ref = next(it)
    g_ref = next(it) if norm else None
    r_ref = next(it) if has_res else None
    o_ref = next(it)
    u_ref = next(it)

    @pl.when(pl.program_id(1) == 0)
    def _():
        x = x_ref[...].astype(F32)
        if norm:
            x = _rms(x, g_ref[...])
        u_ref[...] = x.astype(BF16)

    acc = _dot(u_ref[...], w_ref[...])
    if out_scale != 1.0:
        acc = acc * out_scale
    if has_res:
        acc = acc + r_ref[...]
    o_ref[...] = acc.astype(o_ref.dtype)


def _mm(x, w, l, *, g=None, res=None, out_scale=1.0, out_dtype=F32, tm=512, tn=1024):
    m, k = x.shape
    n = w.shape[2]
    tm = min(tm, m)
    tn = min(tn, n)
    assert m % tm == 0 and n % tn == 0, (m, n, tm, tn)
    in_specs = [pl.BlockSpec((tm, k), lambda i, j: (i, 0)),
                pl.BlockSpec((None, k, tn), lambda i, j: (l, 0, j))]
    args = [x, w]
    if g is not None:
        in_specs.append(pl.BlockSpec((None, 1, k), lambda i, j: (l, 0, 0)))
        args.append(g)
    if res is not None:
        in_specs.append(pl.BlockSpec((tm, tn), lambda i, j: (i, j)))
        args.append(res)
    return pl.pallas_call(
        functools.partial(_mm_body, norm=g is not None, has_res=res is not None, out_scale=out_scale),
        grid=(m // tm, n // tn),
        in_specs=in_specs,
        out_specs=pl.BlockSpec((tm, tn), lambda i, j: (i, j)),
        out_shape=jax.ShapeDtypeStruct((m, n), out_dtype),
        scratch_shapes=[pltpu.VMEM((tm, k), BF16)],
        compiler_params=_cparams("parallel", "arbitrary"),
        name="mm",
    )(*args)


def _win_body(w_ref, zxd_ref, pg_ref):
    zxd_ref[...] = w_ref[:, 0:ZXD_W].astype(BF16)
    pg_ref[...] = w_ref[:, OFF_POOL:OFF_POOL + PG_W].astype(BF16)


def _split_w_in(w_in, tk=256):
    depth, k, n = w_in.shape
    assert n == OFF_POOL + PG_W and k % tk == 0
    return pl.pallas_call(
        _win_body,
        grid=(depth, k // tk),
        in_specs=[pl.BlockSpec((None, tk, n), lambda l, i: (l, i, 0))],
        out_specs=[pl.BlockSpec((None, tk, ZXD_W), lambda l, i: (l, i, 0)),
                   pl.BlockSpec((None, tk, PG_W), lambda l, i: (l, i, 0))],
        out_shape=[jax.ShapeDtypeStruct((depth, k, ZXD_W), BF16),
                   jax.ShapeDtypeStruct((depth, k, PG_W), BF16)],
        compiler_params=_cparams("parallel", "parallel"),
        name="split_w_in",
    )(w_in)


def _kv_body(x_ref, wk_ref, wv_ref, k_ref, v_ref, u_ref):
    @pl.when(pl.program_id(1) == 0)
    def _():
        u_ref[...] = x_ref[...].astype(BF16)

    tm = x_ref.shape[0]
    for w_ref, o_ref in ((wk_ref, k_ref), (wv_ref, v_ref)):
        res = _dot(u_ref[...], w_ref[...])
        for h in range(X_HEADS):
            for half in range(HALVES):
                t = h * HALVES + half
                o_ref[pl.ds(half * X_HEADS + h, tm, stride=SUBLANES), :] = res[:, t * LANES:(t + 1) * LANES]


def _kv_proj(mem, wk, wv, tm=512):
    m, k = mem.shape
    depth = wk.shape[0]
    tm = min(tm, m)
    wspec = pl.BlockSpec((None, k, D_MODEL), lambda i, l: (l, 0, 0))
    ospec = pl.BlockSpec((None, tm * SUBLANES, LANES), lambda i, l: (l, i, 0))
    oshape = jax.ShapeDtypeStruct((depth, m * SUBLANES, LANES), F32)
    return pl.pallas_call(
        _kv_body,
        grid=(m // tm, depth),
        in_specs=[pl.BlockSpec((tm, k), lambda i, l: (i, 0)), wspec, wspec],
        out_specs=[ospec, ospec],
        out_shape=[oshape, oshape],
        scratch_shapes=[pltpu.VMEM((tm, k), BF16)],
        compiler_params=_cparams("parallel", "arbitrary"),
        name="kv_proj",
    )(mem, wk, wv)


def _rmsnorm_body(x_ref, g_ref, o_ref):
    o_ref[...] = _rms(x_ref[...], g_ref[...])


def _rmsnorm(x, g, tm=512):
    m, k = x.shape
    tm = min(tm, m)
    return pl.pallas_call(
        _rmsnorm_body,
        grid=(m // tm,),
        in_specs=[pl.BlockSpec((tm, k), lambda i: (i, 0)), pl.BlockSpec((1, k), lambda i: (0, 0))],
        out_specs=pl.BlockSpec((tm, k), lambda i: (i, 0)),
        out_shape=jax.ShapeDtypeStruct((m, k), F32),
        compiler_params=_cparams("parallel"),
        name="rmsnorm",
    )(x, g.reshape(1, k))


PH = CHUNK // SUBLANES
HB = PH + SUBLANES
CONV_WRAP = (5, 6, 7)


def _blk(v, r):
    return v[r * PH:(r + 1) * PH]


def _mixer_body(x_ref, g_ref, wzxd_ref, wpg_ref, cw_ref, cb_ref, dtb_ref, a_ref, de_ref, nw_ref,
                wsp_ref, pw_ref, ps_ref, wo_ref,
                o_ref, sst_ref, cst_ref, pst_ref,
                perm, xh, xsh, ph, psh1, psh2, st_scr, *, nchunks):
    c = pl.program_id(1)
    L = CHUNK

    @pl.when(c == 0)
    def _():
        for q in range(len(CONV_WRAP)):
            xh[q * HB:q * HB + SUBLANES, :] = jnp.zeros((SUBLANES, CONV_DIM), F32)
        for p in range(SUBLANES):
            ph[p * HB:p * HB + SUBLANES, :] = jnp.zeros((SUBLANES, D_MODEL), F32)
        st_scr[...] = jnp.zeros_like(st_scr)

    n_lt = D_MODEL // LANES
    for t in range(n_lt):
        perm[t] = x_ref[:, t * LANES:(t + 1) * LANES]
    x = jnp.concatenate(
        [jnp.concatenate([perm[t, pl.ds(r, PH, stride=SUBLANES), :] for t in range(n_lt)], axis=1)
         for r in range(SUBLANES)], axis=0)
    u = _mul8(x * lax.rsqrt(jnp.mean(x * x, axis=-1, keepdims=True) + EPS), g_ref[...]).astype(BF16)
    xbc = _dot(u, wzxd_ref[:, OFF_XBC:OFF_DT])
    dtr = _dot(u, wzxd_ref[:, OFF_DT:ZXD_W])
    pu = _dot(u, wpg_ref[:, 0:D_MODEL])
    z = _dot(u, wzxd_ref[:, 0:OFF_XBC])
    ga = _dot(u, wpg_ref[:, D_MODEL:2 * D_MODEL])
    gb = _dot(u, wpg_ref[:, 2 * D_MODEL:3 * D_MODEL])

    for q, p in enumerate(CONV_WRAP):
        xh[q * HB + SUBLANES:(q + 1) * HB, :] = _blk(xbc, p)
        xsh[q * PH:(q + 1) * PH, :] = xh[q * HB + SUBLANES - 1:(q + 1) * HB - 1, :]
    conv_blocks = []
    for r in range(SUBLANES):
        acc = None
        for k in range(CONV_WIDTH):
            rp = r - (CONV_WIDTH - 1) + k
            src = _blk(xbc, rp) if rp >= 0 else xsh[(rp + 3) * PH:(rp + 4) * PH, :]
            term = _mul8(src, cw_ref[k])
            acc = _add8(term, cb_ref[...]) if acc is None else acc + term
        conv_blocks.append(acc)
    conv = jnp.concatenate(conv_blocks, axis=0)
    act = _silu(conv)
    xs = act[:, :D_INNER]

    dt = _softplus(_add8(dtr, dtb_ref[...]))
    a = _mul8(dt, a_ref[...])
    j0 = lax.broadcasted_iota(jnp.int32, (L, L), 0)
    j1 = lax.broadcasted_iota(jnp.int32, (L, L), 1)
    tok = lambda j: ((j & (PH - 1)) << 3) | (j >> 4)
    tri = tok(j0) >= tok(j1)
    ltri = jnp.where(tri, 1.0, 0.0).astype(BF16)
    a_hi, a_mid, a_lo = _split3(a)
    cs = (_dot(ltri, a_hi) + _dot(ltri, a_mid) + _dot(ltri, a_lo)) * LOG2E
    cs_t = cs.T
    dt_t = dt.T
    cs_last = cs[L - 1:L, :]
    wdec = dt * jnp.exp2(cs_last - cs)
    lt64 = j1 < SSD_HEAD_DIM
    lt64_row = lax.broadcasted_iota(jnp.int32, (1, L), 1) < SSD_HEAD_DIM

    y_pairs = []
    for g in range(SSD_GROUPS):
        bm = act[:, D_INNER + g * SSD_STATE:D_INNER + (g + 1) * SSD_STATE]
        cm = act[:, D_INNER + (SSD_GROUPS + g) * SSD_STATE:D_INNER + (SSD_GROUPS + g + 1) * SSD_STATE]
        cb = _dot_nt(cm.astype(BF16), bm.astype(BF16))
        xw_pairs = []
        dec_pairs = []
        for kk in range(4):
            k = g * 4 + kk
            lhs_rows, colbs = [], []
            for h in (2 * k, 2 * k + 1):
                colb = jnp.broadcast_to(cs[:, h:h + 1], (L, L))
                lm = jnp.where(tri, jnp.exp2(colb - cs_t[h:h + 1, :]), 0.0)
                lhs_rows.append(jnp.concatenate([cb * lm * dt_t[h:h + 1, :], cm * jnp.exp2(colb)], axis=1))
                colbs.append(colb)
            lhs = jnp.concatenate(lhs_rows, axis=0).astype(BF16)
            xs_p = xs[:, k * LANES:(k + 1) * LANES]
            st_p = st_scr[:, k * LANES:(k + 1) * LANES]
            both = _dot(lhs, jnp.concatenate([xs_p, st_p], axis=0).astype(BF16))
            y_pairs.append(jnp.where(lt64, both[:L], both[L:]))
            wb = jnp.where(lt64, jnp.broadcast_to(wdec[:, 2 * k:2 * k + 1], (L, L)),
                           jnp.broadcast_to(wdec[:, 2 * k + 1:2 * k + 2], (L, L)))
            xw_pairs.append(xs_p * wb)
            dec_pairs.append(jnp.exp2(jnp.where(lt64_row, colbs[0][L - 1:L, :], colbs[1][L - 1:L, :])))
        xw = jnp.concatenate(xw_pairs, axis=1).astype(BF16)
        dec = jnp.concatenate(dec_pairs, axis=1)
        inc = _dot(bm.T.astype(BF16), xw)
        sl = slice(g * GROUP_W, (g + 1) * GROUP_W)
        st_scr[:, sl] = st_scr[:, sl] * dec + inc

    y = jnp.concatenate(y_pairs, axis=1)
    y = y + _mul8(xs, de_ref[...])
    y = y * _silu(z)
    ya = _mul8(_group_rms(y), nw_ref[...]).astype(BF16)

    for p in range(SUBLANES):
        ph[p * HB + SUBLANES:(p + 1) * HB, :] = _blk(pu, p)
    w_max = POOL_WINDOWS[-1]
    lo_max = (len(POOL_WINDOWS) - 1) * POOL_GW
    for p in range(SUBLANES):
        psh1[p * PH:(p + 1) * PH, :] = ph[p * HB + SUBLANES - 1:(p + 1) * HB - 1, :]
        psh2[p * PH:(p + 1) * PH, :] = ph[p * HB + SUBLANES - 2:(p + 1) * HB - 2, lo_max:lo_max + POOL_GW]
    i_col = lax.broadcasted_iota(jnp.int32, (PH, 1), 0)
    d_blocks = []
    for r in range(SUBLANES):
        pos = c * L + i_col * SUBLANES + r
        outs = []
        for gi, w in enumerate(POOL_WINDOWS):
            lo = gi * POOL_GW
            cur = _blk(pu, r)[:, lo:lo + POOL_GW]
            acc = cur
            for k in range(1, w):
                p, shift = (r - k) % SUBLANES, -((r - k) // SUBLANES)
                if shift == 0:
                    src = _blk(pu, p)[:, lo:lo + POOL_GW]
                elif shift == 1:
                    src = psh1[p * PH:(p + 1) * PH, lo:lo + POOL_GW]
                else:
                    assert shift == 2 and w == w_max
                    src = psh2[p * PH:(p + 1) * PH, :]
                acc = acc + src
            outs.append(acc / jnp.minimum(w, pos + 1).astype(F32) - cur)
        d_blocks.append(jnp.concatenate(outs, axis=1))
    d = jnp.concatenate(d_blocks, axis=0).astype(BF16)

    out = x + _merge(ya, d, ga, gb, wsp_ref, pw_ref, lambda v: _mul8(v, ps_ref[...]), wo_ref)
    for r in range(SUBLANES):
        for t in range(n_lt):
            perm[t, pl.ds(r, PH, stride=SUBLANES), :] = _blk(out, r)[:, t * LANES:(t + 1) * LANES]
    for t in range(n_lt):
        o_ref[:, t * LANES:(t + 1) * LANES] = perm[t]

    @pl.when(c == nchunks - 1)
    def _():
        for k in range(D_INNER // LANES):
            sst_ref[0, k * LANES:(k + 1) * LANES, :] = st_scr[:, k * LANES:(k + 1) * LANES].T
        for q in range(len(CONV_WRAP)):
            cst_ref[0, q:q + 1, :] = xh[(q + 1) * HB - 1:(q + 1) * HB, :]
        for n in range(POOL_BUF):
            t = L - POOL_BUF + n
            row = (t % SUBLANES) * HB + SUBLANES + t // SUBLANES
            pst_ref[0, n:n + 1, :] = ph[row:row + 1, :]

    for q in range(len(CONV_WRAP)):
        xh[q * HB:q * HB + SUBLANES, :] = xh[q * HB + PH:(q + 1) * HB, :]
    for p in range(SUBLANES):
        ph[p * HB:p * HB + SUBLANES, :] = ph[p * HB + PH:(p + 1) * HB, :]


def _mixer_prompt(x, W, l, batch, seq):
    nchunks = seq // CHUNK
    tok = lambda b, c: (b * nchunks + c, 0)
    per_batch = lambda *tail: pl.BlockSpec((1,) + tail, lambda b, c: (b,) + (0,) * len(tail))
    return pl.pallas_call(
        functools.partial(_mixer_body, nchunks=nchunks),
        grid=(batch, nchunks),
        in_specs=[pl.BlockSpec((CHUNK, D_MODEL), tok),
                  _layer_spec(l, (SUBLANES, D_MODEL)),
                  _layer_spec(l, (D_MODEL, ZXD_W), resident=True),
                  _layer_spec(l, (D_MODEL, PG_W), resident=True),
                  _layer_spec(l, (CONV_WIDTH, SUBLANES, CONV_DIM)),
                  _layer_spec(l, (SUBLANES, CONV_DIM)),
                  _layer_spec(l, (SUBLANES, LANES)),
                  _layer_spec(l, (SUBLANES, LANES)),
                  _layer_spec(l, (SUBLANES, D_INNER)),
                  _layer_spec(l, (SUBLANES, D_INNER)),
                  _layer_spec(l, (D_INNER, D_MODEL), resident=True),
                  _layer_spec(l, (len(POOL_WINDOWS), POOL_GW, POOL_GW), resident=True),
                  _layer_spec(l, (SUBLANES, D_MODEL)),
                  _layer_spec(l, (D_MODEL, D_MODEL), resident=True)],
        out_specs=[pl.BlockSpec((CHUNK, D_MODEL), tok),
                   per_batch(D_INNER, SSD_STATE),
                   per_batch(CONV_WIDTH - 1, CONV_DIM),
                   per_batch(POOL_BUF, D_MODEL)],
        out_shape=[jax.ShapeDtypeStruct((batch * seq, D_MODEL), F32),
                   jax.ShapeDtypeStruct((batch, D_INNER, SSD_STATE), F32),
                   jax.ShapeDtypeStruct((batch, CONV_WIDTH - 1, CONV_DIM), F32),
                   jax.ShapeDtypeStruct((batch, POOL_BUF, D_MODEL), F32)],
        scratch_shapes=[pltpu.VMEM((D_MODEL // LANES, CHUNK, LANES), F32),
                        pltpu.VMEM((len(CONV_WRAP) * HB, CONV_DIM), F32),
                        pltpu.VMEM((len(CONV_WRAP) * PH, CONV_DIM), F32),
                        pltpu.VMEM((SUBLANES * HB, D_MODEL), F32),
                        pltpu.VMEM((SUBLANES * PH, D_MODEL), F32),
                        pltpu.VMEM((SUBLANES * PH, POOL_GW), F32),
                        pltpu.VMEM((SSD_STATE, D_INNER), F32)],
        compiler_params=_cparams("parallel", "arbitrary"),
        name="mixer_prompt",
    )(x, W["norm_mix8"], W["w_zxd"], W["w_pg"], W["conv_w8"], W["conv_b8"], W["dt_bias8"], W["a_neg8"],
      W["d_e8"], W["ssd_norm8"], W["w_ssd_proj"], W["pool_w"], W["pool_scale8"], W["w_o"])


def _attn_body(*refs, ssm_tb, n_alias):
    x_ref, g_ref, wq_ref, k_ref, v_ref, wo_ref, s_ref, xdt_ref, dec_ref, act_ref = refs[:10]
    o_ref, so_ref, y_ref = refs[10 + n_alias:]
    _attn_core(x_ref, g_ref, wq_ref, k_ref, v_ref, wo_ref, o_ref)
    _ssm_core(s_ref, xdt_ref, dec_ref, act_ref, so_ref, y_ref, ssm_tb)


def _attn_core(x_ref, g_ref, wq_ref, k_ref, v_ref, wo_ref, o_ref):
    x = x_ref[...]
    u = _rms(x, g_ref[...]).astype(BF16)
    q = (_dot(u, wq_ref[...]) * (X_HEAD_DIM ** -0.5)).astype(BF16)
    outs = []
    head = lambda ref, h: jnp.concatenate(
        [ref[pl.ds(half * X_HEADS + h, MEM_LEN, stride=SUBLANES), :] for half in range(HALVES)],
        axis=1).astype(BF16)
    for h in range(X_HEADS):
        s = _dot_nt(q[:, h * X_HEAD_DIM:(h + 1) * X_HEAD_DIM], head(k_ref, h))
        p = jnp.exp(s - jnp.max(s, axis=-1, keepdims=True))
        p = p / jnp.sum(p, axis=-1, keepdims=True)
        outs.append(_dot(p.astype(BF16), head(v_ref, h)))
    o = jnp.concatenate(outs, axis=1).astype(BF16)
    o_ref[...] = x + _dot(o, wo_ref[...])


def _attn_prompt(x, mk, mv, W, l, batch, seq, states, prev, xdt, dec, act, tm=512):
    tm = min(tm, seq)
    assert seq % tm == 0
    nt = seq // tm
    nsteps = batch * nt
    m = xdt.shape[0]
    assert m % nsteps == 0
    tb = m // nsteps
    tok = lambda b, i: (b * nt + i, 0)
    step3 = lambda b, i: (b * nt + i, 0, 0)
    kv = pl.BlockSpec((None, KV_ROWS, LANES), lambda b, i: (l, b, 0))
    cols = lambda v: v.reshape(nsteps, tb, D_INNER).transpose(0, 2, 1)
    st_spec = pl.BlockSpec((None, tb, D_INNER, SSD_STATE), lambda b, i: (l, b * nt + i, 0, 0))
    col_spec = pl.BlockSpec((1, D_INNER, tb), step3)
    in_specs = [pl.BlockSpec((tm, D_MODEL), tok),
                _layer_spec(l, (1, D_MODEL)),
                _layer_spec(l, (D_MODEL, D_MODEL), resident=True), kv, kv,
                _layer_spec(l, (D_MODEL, D_MODEL), resident=True),
                st_spec, col_spec, col_spec, pl.BlockSpec((1, tb, CONV_DIM), step3)]
    args = [x, W["norm_x"], W["w_xq"], mk, mv, W["w_xo"], states, cols(xdt), cols(dec),
            act.reshape(nsteps, tb, CONV_DIM)]
    aliases = {}
    if prev is not None:
        in_specs.append(pl.BlockSpec(memory_space=pl.ANY))
        args.append(prev)
        aliases = {len(args) - 1: 1}
    x, new_states, y = pl.pallas_call(
        functools.partial(_attn_body, ssm_tb=tb, n_alias=len(aliases)),
        grid=(batch, nt),
        in_specs=in_specs,
        out_specs=[pl.BlockSpec((tm, D_MODEL), tok), st_spec, pl.BlockSpec((1, tb, D_INNER), step3)],
        out_shape=[jax.ShapeDtypeStruct((batch * seq, D_MODEL), F32),
                   jax.ShapeDtypeStruct(states.shape, F32),
                   jax.ShapeDtypeStruct((nsteps, tb, D_INNER), F32)],
        input_output_aliases=aliases,
        compiler_params=_cparams("parallel", "parallel"),
        name="attn_prompt",
    )(*args)
    return x, new_states, y.reshape(m, D_INNER)


def _mlp_body(*refs, att_tb):
    if att_tb:
        x_ref, g_ref, wu_ref, wd_ref, q_ref, k_ref, v_ref, o_ref, so_ref, xm_ref = refs
    else:
        x_ref, g_ref, wu_ref, wd_ref, o_ref, xm_ref = refs

    @pl.when(pl.program_id(1) == 0)
    def _():
        x = x_ref[...]
        xm_ref[...] = _rms(x, g_ref[...]).astype(BF16)
        o_ref[...] = x

    h = jnp.square(jnp.maximum(_dot(xm_ref[...], wu_ref[...]), 0.0)).astype(BF16)
    o_ref[...] += _dot(h, wd_ref[...])
    if att_tb:
        _sattn_core(q_ref, k_ref, v_ref, so_ref, att_tb)


def _mlp(x, W, l, side=None, tm=1024, tf=1024):
    m = x.shape[0]
    tm = min(tm, m)
    assert m % tm == 0
    nk = D_FF // tf
    in_specs = [pl.BlockSpec((tm, D_MODEL), lambda i, k: (i, 0)),
                _layer_spec(l, (1, D_MODEL)),
                pl.BlockSpec((None, D_MODEL, tf), lambda i, k: (l, 0, k)),
                pl.BlockSpec((None, tf, D_MODEL), lambda i, k: (l, k, 0))]
    out_specs = [pl.BlockSpec((tm, D_MODEL), lambda i, k: (i, 0))]
    out_shape = [jax.ShapeDtypeStruct((m, D_MODEL), F32)]
    args = [x, W["norm_mlp"], W["w_up"], W["w_down"]]
    tb = 0
    if side is not None:
        q8, ck, cv = side
        ms = q8.shape[0]
        nsteps = (m // tm) * nk
        assert ms % nsteps == 0
        tb = ms // nsteps
        kv = pl.BlockSpec((None, tb, KV_ROWS, LANES), lambda i, k: (l, i * nk + k, 0, 0))
        qo = pl.BlockSpec((tb, SUBLANES, LANES), lambda i, k: (i * nk + k, 0, 0))
        in_specs += [qo, kv, kv]
        out_specs.append(qo)
        out_shape.append(jax.ShapeDtypeStruct((ms, SUBLANES, LANES), F32))
        args += [q8, ck, cv]
    outs = pl.pallas_call(
        functools.partial(_mlp_body, att_tb=tb),
        grid=(m // tm, nk),
        in_specs=in_specs,
        out_specs=out_specs,
        out_shape=out_shape,
        scratch_shapes=[pltpu.VMEM((tm, D_MODEL), BF16)],
        compiler_params=_cparams("parallel", "arbitrary"),
        name="mlp",
    )(*args)
    return outs[0] if side is None else outs


STEP_TB = 32


def _step_body(x_ref, g_ref, w_ref, wpg_ref, cst_ref, cw_ref, cb_ref, dtb_ref, a_ref, pst_ref, *rest, start_pos):
    ncst_ref, npst_ref, act_ref, xdt_ref, dec_ref, d_ref, z_ref, pg_ref = rest[-8:]
    x = x_ref[...]
    u = _mul8(x * lax.rsqrt(jnp.mean(x * x, axis=-1, keepdims=True) + EPS), g_ref[...]).astype(BF16)
    z_ref[...] = _dot(u, w_ref[:, 0:OFF_XBC])
    xbc = _dot(u, w_ref[:, OFF_XBC:OFF_DT])
    dtr = _dot(u, w_ref[:, OFF_DT:ZXD_W])
    pg = _dot(u, wpg_ref[...])
    pg_ref[...] = pg
    pu = pg[:, 0:D_MODEL]

    conv = None
    for k in range(CONV_WIDTH):
        src = cst_ref[k] if k < CONV_WIDTH - 1 else xbc
        term = _mul8(src, cw_ref[k])
        conv = _add8(term, cb_ref[...]) if conv is None else conv + term
    for k in range(CONV_WIDTH - 2):
        ncst_ref[k] = cst_ref[k + 1]
    ncst_ref[CONV_WIDTH - 2] = xbc
    act = _silu(conv)
    act_ref[...] = act

    dt = _softplus(_add8(dtr, dtb_ref[...]))
    dec = jnp.exp(_mul8(dt, a_ref[...]))
    lt64 = lax.broadcasted_iota(jnp.int32, (1, LANES), 1) < SSD_HEAD_DIM
    m = dt.shape[0]
    for k in range(D_INNER // LANES):
        dt_e = jnp.where(lt64, jnp.broadcast_to(dt[:, 2 * k:2 * k + 1], (m, LANES)),
                         jnp.broadcast_to(dt[:, 2 * k + 1:2 * k + 2], (m, LANES)))
        dec_e = jnp.where(lt64, jnp.broadcast_to(dec[:, 2 * k:2 * k + 1], (m, LANES)),
                          jnp.broadcast_to(dec[:, 2 * k + 1:2 * k + 2], (m, LANES)))
        xdt_ref[:, k * LANES:(k + 1) * LANES] = act[:, k * LANES:(k + 1) * LANES] * dt_e
        dec_ref[:, k * LANES:(k + 1) * LANES] = dec_e

    outs = []
    for gi, w in enumerate(POOL_WINDOWS):
        lo = gi * POOL_GW
        cur = pu[:, lo:lo + POOL_GW]
        acc = cur
        for k in range(1, w):
            acc = acc + pst_ref[POOL_BUF - k, :, lo:lo + POOL_GW]
        outs.append(acc / float(min(w, start_pos + 1)) - cur)
    d_ref[...] = jnp.concatenate(outs, axis=1).astype(d_ref.dtype)
    for j in range(POOL_BUF - 1):
        npst_ref[j] = pst_ref[j + 1]
    npst_ref[POOL_BUF - 1] = pu


def _sample_step(x, cst_t, pst_t, prev, W, l, start_pos):
    m = x.shape[0]
    tm = min(STEP_TB, m)
    rows = lambda width: pl.BlockSpec((tm, width), lambda i: (i, 0))
    state = lambda r, w: pl.BlockSpec((None, r, tm, w), lambda i: (l, 0, i, 0))
    widths = [CONV_DIM, D_INNER, D_INNER, D_MODEL, D_INNER, PG_W]
    dtypes = [F32, F32, F32, BF16, F32, F32]
    in_specs = [rows(D_MODEL), _layer_spec(l, (SUBLANES, D_MODEL)),
                _layer_spec(l, (D_MODEL, ZXD_W), resident=True), _layer_spec(l, (D_MODEL, PG_W), resident=True),
                state(CONV_WIDTH - 1, CONV_DIM), _layer_spec(l, (CONV_WIDTH, SUBLANES, CONV_DIM)),
                _layer_spec(l, (SUBLANES, CONV_DIM)), _layer_spec(l, (SUBLANES, LANES)),
                _layer_spec(l, (SUBLANES, LANES)), state(POOL_BUF, D_MODEL)]
    args = [x, W["norm_mix8"], W["w_zxd"], W["w_pg"], cst_t, W["conv_w8"], W["conv_b8"], W["dt_bias8"],
            W["a_neg8"], pst_t]
    aliases = {}
    if prev is not None:
        aliases = {len(args): 0, len(args) + 1: 1}
        in_specs += [pl.BlockSpec(memory_space=pl.ANY)] * 2
        args += list(prev)
    return pl.pallas_call(
        functools.partial(_step_body, start_pos=start_pos),
        grid=(m // tm,),
        in_specs=in_specs,
        out_specs=[state(CONV_WIDTH - 1, CONV_DIM), state(POOL_BUF, D_MODEL)] + [rows(w) for w in widths],
        out_shape=[jax.ShapeDtypeStruct(cst_t.shape, F32), jax.ShapeDtypeStruct(pst_t.shape, F32)]
        + [jax.ShapeDtypeStruct((m, w), dt) for w, dt in zip(widths, dtypes)],
        input_output_aliases=aliases,
        compiler_params=_cparams("parallel"),
        name="sample_step",
    )(*args)


def _ssm_core(s_ref, xdt_ref, dec_ref, act_ref, so_ref, y_ref, tb):
    for j in range(tb):
        for g in range(SSD_GROUPS):
            rows = slice(g * GROUP_W, (g + 1) * GROUP_W)
            b_row = act_ref[0, j:j + 1, D_INNER + g * SSD_STATE:D_INNER + (g + 1) * SSD_STATE]
            c_row = act_ref[0, j:j + 1, D_INNER + (SSD_GROUPS + g) * SSD_STATE:
                            D_INNER + (SSD_GROUPS + g + 1) * SSD_STATE]
            sn = s_ref[j, rows, :] * dec_ref[0, rows, j:j + 1] + xdt_ref[0, rows, j:j + 1] * b_row
            so_ref[j, rows, :] = sn
            c8 = jnp.broadcast_to(c_row, (SUBLANES, SSD_STATE)).astype(BF16)
            y_ref[0, j:j + 1, rows] = _dot_nt(c8, sn.astype(BF16))[0:1, :]


def _gate_body(y_ref, act_ref, z_ref, de_ref, nw_ref, o_ref):
    y = y_ref[...] + _mul8(act_ref[:, :D_INNER], de_ref[...])
    y = y * _silu(z_ref[...])
    o_ref[...] = _mul8(_group_rms(y), nw_ref[...]).astype(o_ref.dtype)


def _sample_gate(y, act, z, W, l):
    m = y.shape[0]
    full = lambda *shape: pl.BlockSpec(shape, lambda i: (0,) * len(shape))
    return pl.pallas_call(
        _gate_body,
        grid=(1,),
        in_specs=[full(m, D_INNER), full(m, CONV_DIM), full(m, D_INNER),
                  _layer_spec(l, (SUBLANES, D_INNER)), _layer_spec(l, (SUBLANES, D_INNER))],
        out_specs=full(m, D_INNER),
        out_shape=jax.ShapeDtypeStruct((m, D_INNER), BF16),
        compiler_params=_cparams("arbitrary"),
        name="sample_gate",
    )(y, act, z, W["d_e8"], W["ssd_norm8"])


def _mixs_body(ya_ref, d_ref, ga_ref, gb_ref, x_ref, wsp_ref, pw_ref, ps_ref, wo_ref, o_ref):
    o_ref[...] = x_ref[...] + _merge(ya_ref[...], d_ref[...], ga_ref[...], gb_ref[...],
                                     wsp_ref, pw_ref, lambda v: _mul8(v, ps_ref[...]), wo_ref)


def _mix_sample(ya, d, pg, x, W, l):
    m = x.shape[0]
    blk = lambda width, j: pl.BlockSpec((m, width), lambda i: (0, j))
    return pl.pallas_call(
        _mixs_body,
        grid=(1,),
        in_specs=[blk(D_INNER, 0), blk(D_MODEL, 0), blk(D_MODEL, 1), blk(D_MODEL, 2), blk(D_MODEL, 0),
                  _layer_spec(l, (D_INNER, D_MODEL)), _layer_spec(l, (len(POOL_WINDOWS), POOL_GW, POOL_GW)),
                  _layer_spec(l, (SUBLANES, D_MODEL)), _layer_spec(l, (D_MODEL, D_MODEL))],
        out_specs=blk(D_MODEL, 0),
        out_shape=jax.ShapeDtypeStruct((m, D_MODEL), F32),
        compiler_params=_cparams("arbitrary"),
        name="mix_sample",
    )(ya, d, pg, pg, x, W["w_ssd_proj"], W["pool_w"], W["pool_scale8"], W["w_o"])


KV_ROWS = MEM_LEN * SUBLANES
HALVES = X_HEAD_DIM // LANES
assert HALVES * X_HEADS == SUBLANES


def _sattn_core(q_ref, k_ref, v_ref, o_ref, tb):
    for j in range(tb):
        k3 = k_ref[j].reshape(MEM_LEN, SUBLANES, LANES)
        v3 = v_ref[j].reshape(MEM_LEN, SUBLANES, LANES)
        prod = k3 * q_ref[j][None]
        prod = prod + pltpu.roll(prod, X_HEADS, axis=1)
        s = jnp.sum(prod, axis=-1, keepdims=True)
        e = jnp.exp(s - jnp.max(s, axis=0, keepdims=True))
        pr = e / jnp.sum(e, axis=0, keepdims=True)
        o_ref[j] = jnp.sum(pr * v3, axis=0)


def _kv_rows(c):
    depth, m = c.shape[:2]
    c = c.reshape(depth, m, MEM_LEN, X_HEADS, HALVES, LANES).transpose(0, 1, 2, 4, 3, 5)
    return c.reshape(depth, m, KV_ROWS, LANES)


def _kv_unrows(r, batch):
    depth = r.shape[0]
    r = r.reshape(depth, batch, MEM_LEN, HALVES, X_HEADS, LANES).transpose(0, 1, 2, 4, 3, 5)
    return r.reshape(depth, batch, MEM_LEN, X_HEADS, X_HEAD_DIM)


def _q_rows(q):
    m = q.shape[0]
    return q.reshape(m, X_HEADS, HALVES, LANES).transpose(0, 2, 1, 3).reshape(m, SUBLANES, LANES)


def _o_cols(o8):
    m = o8.shape[0]
    return o8.reshape(m, HALVES, X_HEADS, LANES).transpose(0, 2, 1, 3).reshape(m, D_MODEL)


def _prep_weights(norm_mix, w_in, conv_w, conv_b, dt_bias, a_log, d_skip, ssd_norm, w_ssd_proj, pool_w,
                  pool_scale, w_o, norm_x, w_xq, w_xk, w_xv, w_xo, norm_mlp, w_up, w_down):
    depth = w_in.shape[0]
    lane_pad = lambda v: jnp.concatenate([v, jnp.zeros((depth, LANES - SSD_HEADS), F32)], axis=1)
    rep8 = lambda v: jnp.broadcast_to(v[:, None, :], (depth, SUBLANES, v.shape[-1]))
    row = lambda v: v[:, None, :]
    w_zxd, w_pg = _split_w_in(w_in)
    return dict(
        norm_mix8=rep8(norm_mix),
        w_zxd=w_zxd,
        w_pg=w_pg,
        conv_w8=jnp.broadcast_to(conv_w[:, :, None, :], (depth, CONV_WIDTH, SUBLANES, CONV_DIM)),
        conv_b8=rep8(conv_b),
        dt_bias8=rep8(lane_pad(dt_bias)),
        a_neg8=rep8(lane_pad(-jnp.exp(a_log))),
        d_e8=rep8(jnp.repeat(d_skip, SSD_HEAD_DIM, axis=1)),
        ssd_norm8=rep8(ssd_norm),
        w_ssd_proj=w_ssd_proj.astype(BF16),
        pool_w=pool_w.astype(BF16), pool_scale8=rep8(pool_scale),
        w_o=w_o.astype(BF16), norm_x=row(norm_x), w_xq=w_xq.astype(BF16),
        w_xk=w_xk.astype(BF16), w_xv=w_xv.astype(BF16), w_xo=w_xo.astype(BF16),
        norm_mlp=row(norm_mlp), w_up=w_up.astype(BF16), w_down=w_down.astype(BF16))


def _layer(hp, hs, mk, mv, states, new_states, cst_t, pst_t, new_hist, ck, cv, W, l, batch, seq):
    ncst, npst, act, xdt, dec, d, z, pg = _sample_step(hs, cst_t, pst_t, new_hist, W, l, PAST_LEN)
    hp, sst, cst, pst = _mixer_prompt(hp, W, l, batch, seq)
    hp, new_states, y = _attn_prompt(hp, mk, mv, W, l, batch, seq, states, new_states, xdt, dec, act)
    ya = _sample_gate(y, act, z, W, l)
    hs = _mix_sample(ya, d, pg, hs, W, l)
    q = _mm(hs, W["w_xq"], l, g=W["norm_x"], out_scale=X_HEAD_DIM ** -0.5)
    hp, o8 = _mlp(hp, W, l, side=(_q_rows(q), ck, cv))
    hs = _mm(_o_cols(o8), W["w_xo"], l, res=hs)
    hs = _mlp(hs, W, l)
    return hp, hs, new_states, (ncst, npst), (sst, cst, pst)


def kernel(x_prompt, x_sample, mem_prompt, state_ssm, state_conv, state_pool, cache_mem_k, cache_mem_v,
           norm_mix, w_in, conv_w, conv_b, dt_bias, a_log, d_skip, ssd_norm, w_ssd_proj, pool_w, pool_scale,
           w_o, norm_x, w_xq, w_xk, w_xv, w_xo, norm_mlp, w_up, w_down, norm_final):
    bp, seq, _ = x_prompt.shape
    bs = x_sample.shape[0]
    depth = w_in.shape[0]
    assert x_sample.shape[1] == 1 and seq % CHUNK == 0
    W = _prep_weights(norm_mix, w_in, conv_w, conv_b, dt_bias, a_log, d_skip, ssd_norm, w_ssd_proj, pool_w,
                      pool_scale, w_o, norm_x, w_xq, w_xk, w_xv, w_xo, norm_mlp, w_up, w_down)
    hp = x_prompt.reshape(bp * seq, D_MODEL)
    hs = x_sample.reshape(bs, D_MODEL)
    mem = mem_prompt.reshape(bp * MEM_LEN, D_MODEL)
    mk, mv = _kv_proj(mem, W["w_xk"], W["w_xv"])
    states = state_ssm.reshape(depth, bs, D_INNER, SSD_STATE)
    ck = _kv_rows(cache_mem_k)
    cv = _kv_rows(cache_mem_v)
    by_row = lambda v: v.transpose(0, 2, 1, 3)
    cst_t, pst_t = by_row(state_conv), by_row(state_pool)
    new_states = new_hist = None
    outs = [[] for _ in range(3)]
    for l in range(depth):
        hp, hs, new_states, new_hist, (sst, cst, pst) = _layer(
            hp, hs, mk, mv, states, new_states, cst_t, pst_t, new_hist, ck, cv, W, l, bp, seq)
        outs[0].append(sst.reshape(bp, SSD_HEADS, SSD_HEAD_DIM, SSD_STATE))
        outs[1].append(cst)
        outs[2].append(pst)
    y_prompt = _rmsnorm(hp, norm_final).reshape(bp, seq, D_MODEL)
    y_sample = _rmsnorm(hs, norm_final).reshape(bs, 1, D_MODEL)
    ssm_p, conv_p, pool_p = (jnp.stack(o) for o in outs)
    return (y_prompt, y_sample, ssm_p, conv_p, pool_p, _kv_unrows(mk, bp), _kv_unrows(mv, bp),
            new_states.reshape(depth, bs, SSD_HEADS, SSD_HEAD_DIM, SSD_STATE),
            by_row(new_hist[0]), by_row(new_hist[1]))
```

```python
import functools
import math

import jax
import jax.numpy as jnp
from jax import lax
from jax.experimental import pallas as pl
from jax.experimental.pallas import tpu as pltpu

F32 = jnp.float32
BF16 = jnp.bfloat16

D_MODEL = 1024
D_INNER = 2048
SSD_HEAD_DIM = 64
SSD_HEADS = 32
SSD_GROUPS = 4
SSD_STATE = 128
GROUP_W = D_INNER // SSD_GROUPS
CONV_WIDTH = 4
CONV_DIM = D_INNER + 2 * SSD_GROUPS * SSD_STATE
CHUNK = 128
POOL_WINDOWS = (2, 4, 8, 16)
POOL_GW = 256
POOL_BUF = 15
MEM_LEN = 256
X_HEADS = 4
X_HEAD_DIM = 256
D_FF = 4096
EPS = 1e-6
PAST_LEN = 16384
LOG2E = math.log2(math.e)
OFF_XBC = D_INNER
OFF_DT = OFF_XBC + CONV_DIM
OFF_POOL = OFF_DT + SSD_HEADS
LANES = 128
SUBLANES = 8
ZXD_W = OFF_DT + LANES
PG_W = 3 * D_MODEL
VMEM_LIMIT = 52 * 1024 * 1024


def _cparams(*sem):
    return pltpu.CompilerParams(dimension_semantics=sem, vmem_limit_bytes=VMEM_LIMIT)


def _layer_spec(l, tail, resident=False):
    mode = dict(pipeline_mode=pl.Buffered(1)) if resident else {}
    return pl.BlockSpec((None,) + tuple(tail), lambda *_: (l,) + (0,) * len(tail), **mode)


def _dot(a, b):
    return jnp.dot(a, b, preferred_element_type=F32)


def _dot_nt(a, b):
    return lax.dot_general(a, b, (((1,), (1,)), ((), ())), preferred_element_type=F32)


def _split3(x):
    hi = x.astype(BF16)
    r1 = x - hi.astype(F32)
    mid = r1.astype(BF16)
    lo = (r1 - mid.astype(F32)).astype(BF16)
    return hi, mid, lo


def _silu(x):
    return x * jax.nn.sigmoid(x)


def _softplus(x):
    return jnp.maximum(x, 0.0) + jnp.log1p(jnp.exp(-jnp.abs(x)))


def _rows8(v, p8, op):
    n = v.shape[-1]
    return op(v.reshape(-1, SUBLANES, n), p8[None]).reshape(v.shape)


def _mul8(v, p8):
    return _rows8(v, p8, jnp.multiply)


def _add8(v, p8):
    return _rows8(v, p8, jnp.add)


def _rms(x, g):
    return x * lax.rsqrt(jnp.mean(x * x, axis=-1, keepdims=True) + EPS) * g


def _group_rms(y):
    parts = []
    for g in range(SSD_GROUPS):
        yg = y[:, g * GROUP_W:(g + 1) * GROUP_W]
        parts.append(yg * lax.rsqrt(jnp.mean(yg * yg, axis=-1, keepdims=True) + EPS))
    return jnp.concatenate(parts, axis=1)


def _merge(ya, d, ga, gb, wsp_ref, pw_ref, ps, wo_ref):
    ssd = _dot(ya, wsp_ref[...])
    pool = jnp.concatenate(
        [_dot(d[:, g * POOL_GW:(g + 1) * POOL_GW], pw_ref[g]) for g in range(len(POOL_WINDOWS))], axis=1)
    merged = jax.nn.sigmoid(ga) * ssd + jax.nn.sigmoid(gb) * ps(pool)
    return _dot(merged.astype(BF16), wo_ref[...])


def _mm_body(*refs, norm, has_res, out_scale):
    it = iter(refs)
    x_ref = next(it)
    w_ref = next(it)
    g_ref = next(it) if norm else None
    r_ref = next(it) if has_res else None
    o_ref = next(it)
    u_ref = next(it)

    @pl.when(pl.program_id(1) == 0)
    def _():
        x = x_ref[...].astype(F32)
        if norm:
            x = _rms(x, g_ref[...])
        u_ref[...] = x.astype(BF16)

    acc = _dot(u_ref[...], w_ref[...])
    if out_scale != 1.0:
        acc = acc * out_scale
    if has_res:
        acc = acc + r_ref[...]
    o_ref[...] = acc.astype(o_ref.dtype)


def _mm(x, w, l, *, g=None, res=None, out_scale=1.0, out_dtype=F32, tm=512, tn=1024):
    m, k = x.shape
    n = w.shape[2]
    tm = min(tm, m)
    tn = min(tn, n)
    assert m % tm == 0 and n % tn == 0, (m, n, tm, tn)
    in_specs = [pl.BlockSpec((tm, k), lambda i, j: (i, 0)),
                pl.BlockSpec((None, k, tn), lambda i, j: (l, 0, j))]
    args = [x, w]
    if g is not None:
        in_specs.append(pl.BlockSpec((None, 1, k), lambda i, j: (l, 0, 0)))
        args.append(g)
    if res is not None:
        in_specs.append(pl.BlockSpec((tm, tn), lambda i, j: (i, j)))
        args.append(res)
    return pl.pallas_call(
        functools.partial(_mm_body, norm=g is not None, has_res=res is not None, out_scale=out_scale),
        grid=(m // tm, n // tn),
        in_specs=in_specs,
        out_specs=pl.BlockSpec((tm, tn), lambda i, j: (i, j)),
        out_shape=jax.ShapeDtypeStruct((m, n), out_dtype),
        scratch_shapes=[pltpu.VMEM((tm, k), BF16)],
        compiler_params=_cparams("parallel", "arbitrary"),
        name="mm",
    )(*args)


def _kv_body(x_ref, wk_ref, wv_ref, k_ref, v_ref, u_ref):
    @pl.when(pl.program_id(1) == 0)
    def _():
        u_ref[...] = x_ref[...].astype(BF16)

    tm = x_ref.shape[0]
    for w_ref, o_ref in ((wk_ref, k_ref), (wv_ref, v_ref)):
        res = _dot(u_ref[...], w_ref[...])
        for h in range(X_HEADS):
            for half in range(HALVES):
                t = h * HALVES + half
                o_ref[pl.ds(half * X_HEADS + h, tm, stride=SUBLANES), :] = res[:, t * LANES:(t + 1) * LANES]


def _kv_proj(mem, wk, wv, tm=512):
    m, k = mem.shape
    depth = wk.shape[0]
    tm = min(tm, m)
    wspec = pl.BlockSpec((None, k, D_MODEL), lambda i, l: (l, 0, 0))
    ospec = pl.BlockSpec((None, tm * SUBLANES, LANES), lambda i, l: (l, i, 0))
    oshape = jax.ShapeDtypeStruct((depth, m * SUBLANES, LANES), F32)
    return pl.pallas_call(
        _kv_body,
        grid=(m // tm, depth),
        in_specs=[pl.BlockSpec((tm, k), lambda i, l: (i, 0)), wspec, wspec],
        out_specs=[ospec, ospec],
        out_shape=[oshape, oshape],
        scratch_shapes=[pltpu.VMEM((tm, k), BF16)],
        compiler_params=_cparams("parallel", "arbitrary"),
        name="kv_proj",
    )(mem, wk, wv)


PH = CHUNK // SUBLANES
HB = PH + SUBLANES
CONV_WRAP = (5, 6, 7)


def _blk(v, r):
    return v[r * PH:(r + 1) * PH]


def _mixer_body(x_ref, g_ref, wzxd_ref, wpg_ref, cw_ref, cb_ref, dtb_ref, a_ref, de_ref, nw_ref,
                wsp_ref, pw_ref, ps_ref, wo_ref,
                o_ref, sst_ref, cst_ref, pst_ref,
                perm, xh, xsh, ph, psh1, psh2, st_scr, *, nchunks):
    c = pl.program_id(1)
    L = CHUNK

    @pl.when(c == 0)
    def _():
        for q in range(len(CONV_WRAP)):
            xh[q * HB:q * HB + SUBLANES, :] = jnp.zeros((SUBLANES, CONV_DIM), F32)
        for p in range(SUBLANES):
            ph[p * HB:p * HB + SUBLANES, :] = jnp.zeros((SUBLANES, D_MODEL), F32)
        st_scr[...] = jnp.zeros_like(st_scr)

    n_lt = D_MODEL // LANES
    for t in range(n_lt):
        perm[t] = x_ref[:, t * LANES:(t + 1) * LANES]
    x = jnp.concatenate(
        [jnp.concatenate([perm[t, pl.ds(r, PH, stride=SUBLANES), :] for t in range(n_lt)], axis=1)
         for r in range(SUBLANES)], axis=0)
    u = _mul8(x * lax.rsqrt(jnp.mean(x * x, axis=-1, keepdims=True) + EPS), g_ref[...]).astype(BF16)
    xbc = _dot(u, wzxd_ref[:, OFF_XBC:OFF_DT])
    dtr = _dot(u, wzxd_ref[:, OFF_DT:ZXD_W])
    pu = _dot(u, wpg_ref[:, 0:D_MODEL])
    z = _dot(u, wzxd_ref[:, 0:OFF_XBC])
    ga = _dot(u, wpg_ref[:, D_MODEL:2 * D_MODEL])
    gb = _dot(u, wpg_ref[:, 2 * D_MODEL:3 * D_MODEL])

    for q, p in enumerate(CONV_WRAP):
        xh[q * HB + SUBLANES:(q + 1) * HB, :] = _blk(xbc, p)
        xsh[q * PH:(q + 1) * PH, :] = xh[q * HB + SUBLANES - 1:(q + 1) * HB - 1, :]
    conv_blocks = []
    for r in range(SUBLANES):
        acc = None
        for k in range(CONV_WIDTH):
            rp = r - (CONV_WIDTH - 1) + k
            src = _blk(xbc, rp) if rp >= 0 else xsh[(rp + 3) * PH:(rp + 4) * PH, :]
            term = _mul8(src, cw_ref[k])
            acc = _add8(term, cb_ref[...]) if acc is None else acc + term
        conv_blocks.append(acc)
    conv = jnp.concatenate(conv_blocks, axis=0)
    act = _silu(conv)
    xs = act[:, :D_INNER]

    dt = _softplus(_add8(dtr, dtb_ref[...]))
    a = _mul8(dt, a_ref[...])
    j0 = lax.broadcasted_iota(jnp.int32, (L, L), 0)
    j1 = lax.broadcasted_iota(jnp.int32, (L, L), 1)
    tok = lambda j: ((j & (PH - 1)) << 3) | (j >> 4)
    tri = tok(j0) >= tok(j1)
    ltri = jnp.where(tri, 1.0, 0.0).astype(BF16)
    a_hi, a_mid, a_lo = _split3(a)
    cs = (_dot(ltri, a_hi) + _dot(ltri, a_mid) + _dot(ltri, a_lo)) * LOG2E
    cs_t = cs.T
    dt_t = dt.T
    cs_last = cs[L - 1:L, :]
    wdec = dt * jnp.exp2(cs_last - cs)
    lt64 = j1 < SSD_HEAD_DIM
    lt64_row = lax.broadcasted_iota(jnp.int32, (1, L), 1) < SSD_HEAD_DIM

    y_pairs = []
    for g in range(SSD_GROUPS):
        bm = act[:, D_INNER + g * SSD_STATE:D_INNER + (g + 1) * SSD_STATE]
        cm = act[:, D_INNER + (SSD_GROUPS + g) * SSD_STATE:D_INNER + (SSD_GROUPS + g + 1) * SSD_STATE]
        cb = _dot_nt(cm.astype(BF16), bm.astype(BF16))
        xw_pairs = []
        dec_pairs = []
        for kk in range(4):
            k = g * 4 + kk
            lhs_rows, colbs = [], []
            for h in (2 * k, 2 * k + 1):
                colb = jnp.broadcast_to(cs[:, h:h + 1], (L, L))
                lm = jnp.where(tri, jnp.exp2(colb - cs_t[h:h + 1, :]), 0.0)
                lhs_rows.append(jnp.concatenate([cb * lm * dt_t[h:h + 1, :], cm * jnp.exp2(colb)], axis=1))
                colbs.append(colb)
            lhs = jnp.concatenate(lhs_rows, axis=0).astype(BF16)
            xs_p = xs[:, k * LANES:(k + 1) * LANES]
            st_p = st_scr[:, k * LANES:(k + 1) * LANES]
            both = _dot(lhs, jnp.concatenate([xs_p, st_p], axis=0).astype(BF16))
            y_pairs.append(jnp.where(lt64, both[:L], both[L:]))
            wb = jnp.where(lt64, jnp.broadcast_to(wdec[:, 2 * k:2 * k + 1], (L, L)),
                           jnp.broadcast_to(wdec[:, 2 * k + 1:2 * k + 2], (L, L)))
            xw_pairs.append(xs_p * wb)
            dec_pairs.append(jnp.exp2(jnp.where(lt64_row, colbs[0][L - 1:L, :], colbs[1][L - 1:L, :])))
        xw = jnp.concatenate(xw_pairs, axis=1).astype(BF16)
        dec = jnp.concatenate(dec_pairs, axis=1)
        inc = _dot(bm.T.astype(BF16), xw)
        sl = slice(g * GROUP_W, (g + 1) * GROUP_W)
        st_scr[:, sl] = st_scr[:, sl] * dec + inc

    y = jnp.concatenate(y_pairs, axis=1)
    y = y + _mul8(xs, de_ref[...])
    y = y * _silu(z)
    ya = _mul8(_group_rms(y), nw_ref[...]).astype(BF16)

    for p in range(SUBLANES):
        ph[p * HB + SUBLANES:(p + 1) * HB, :] = _blk(pu, p)
    w_max = POOL_WINDOWS[-1]
    lo_max = (len(POOL_WINDOWS) - 1) * POOL_GW
    for p in range(SUBLANES):
        psh1[p * PH:(p + 1) * PH, :] = ph[p * HB + SUBLANES - 1:(p + 1) * HB - 1, :]
        psh2[p * PH:(p + 1) * PH, :] = ph[p * HB + SUBLANES - 2:(p + 1) * HB - 2, lo_max:lo_max + POOL_GW]
    i_col = lax.broadcasted_iota(jnp.int32, (PH, 1), 0)
    d_blocks = []
    for r in range(SUBLANES):
        pos = c * L + i_col * SUBLANES + r
        outs = []
        for gi, w in enumerate(POOL_WINDOWS):
            lo = gi * POOL_GW
            cur = _blk(pu, r)[:, lo:lo + POOL_GW]
            acc = cur
            for k in range(1, w):
                p, shift = (r - k) % SUBLANES, -((r - k) // SUBLANES)
                if shift == 0:
                    src = _blk(pu, p)[:, lo:lo + POOL_GW]
                elif shift == 1:
                    src = psh1[p * PH:(p + 1) * PH, lo:lo + POOL_GW]
                else:
                    assert shift == 2 and w == w_max
                    src = psh2[p * PH:(p + 1) * PH, :]
                acc = acc + src
            outs.append(acc / jnp.minimum(w, pos + 1).astype(F32) - cur)
        d_blocks.append(jnp.concatenate(outs, axis=1))
    d = jnp.concatenate(d_blocks, axis=0).astype(BF16)

    out = x + _merge(ya, d, ga, gb, wsp_ref, pw_ref, lambda v: _mul8(v, ps_ref[...]), wo_ref)
    for r in range(SUBLANES):
        for t in range(n_lt):
            perm[t, pl.ds(r, PH, stride=SUBLANES), :] = _blk(out, r)[:, t * LANES:(t + 1) * LANES]
    for t in range(n_lt):
        o_ref[:, t * LANES:(t + 1) * LANES] = perm[t]

    @pl.when(c == nchunks - 1)
    def _():
        for k in range(D_INNER // LANES):
            sst_ref[0, k * LANES:(k + 1) * LANES, :] = st_scr[:, k * LANES:(k + 1) * LANES].T
        for q in range(len(CONV_WRAP)):
            cst_ref[0, q:q + 1, :] = xh[(q + 1) * HB - 1:(q + 1) * HB, :]
        for n in range(POOL_BUF):
            t = L - POOL_BUF + n
            row = (t % SUBLANES) * HB + SUBLANES + t // SUBLANES
            pst_ref[0, n:n + 1, :] = ph[row:row + 1, :]

    for q in range(len(CONV_WRAP)):
        xh[q * HB:q * HB + SUBLANES, :] = xh[q * HB + PH:(q + 1) * HB, :]
    for p in range(SUBLANES):
        ph[p * HB:p * HB + SUBLANES, :] = ph[p * HB + PH:(p + 1) * HB, :]


def _mixer_prompt(x, W, l, batch, seq):
    nchunks = seq // CHUNK
    tok = lambda b, c: (b * nchunks + c, 0)
    per_batch = lambda *tail: pl.BlockSpec((1,) + tail, lambda b, c: (b,) + (0,) * len(tail))
    return pl.pallas_call(
        functools.partial(_mixer_body, nchunks=nchunks),
        grid=(batch, nchunks),
        in_specs=[pl.BlockSpec((CHUNK, D_MODEL), tok),
                  _layer_spec(l, (SUBLANES, D_MODEL)),
                  _layer_spec(l, (D_MODEL, ZXD_W), resident=True),
                  _layer_spec(l, (D_MODEL, PG_W), resident=True),
                  _layer_spec(l, (CONV_WIDTH, SUBLANES, CONV_DIM)),
                  _layer_spec(l, (SUBLANES, CONV_DIM)),
                  _layer_spec(l, (SUBLANES, LANES)),
                  _layer_spec(l, (SUBLANES, LANES)),
                  _layer_spec(l, (SUBLANES, D_INNER)),
                  _layer_spec(l, (SUBLANES, D_INNER)),
                  _layer_spec(l, (D_INNER, D_MODEL), resident=True),
                  _layer_spec(l, (len(POOL_WINDOWS), POOL_GW, POOL_GW), resident=True),
                  _layer_spec(l, (SUBLANES, D_MODEL)),
                  _layer_spec(l, (D_MODEL, D_MODEL), resident=True)],
        out_specs=[pl.BlockSpec((CHUNK, D_MODEL), tok),
                   per_batch(D_INNER, SSD_STATE),
                   per_batch(CONV_WIDTH - 1, CONV_DIM),
                   per_batch(POOL_BUF, D_MODEL)],
        out_shape=[jax.ShapeDtypeStruct((batch * seq, D_MODEL), F32),
                   jax.ShapeDtypeStruct((batch, D_INNER, SSD_STATE), F32),
                   jax.ShapeDtypeStruct((batch, CONV_WIDTH - 1, CONV_DIM), F32),
                   jax.ShapeDtypeStruct((batch, POOL_BUF, D_MODEL), F32)],
        scratch_shapes=[pltpu.VMEM((D_MODEL // LANES, CHUNK, LANES), F32),
                        pltpu.VMEM((len(CONV_WRAP) * HB, CONV_DIM), F32),
                        pltpu.VMEM((len(CONV_WRAP) * PH, CONV_DIM), F32),
                        pltpu.VMEM((SUBLANES * HB, D_MODEL), F32),
                        pltpu.VMEM((SUBLANES * PH, D_MODEL), F32),
                        pltpu.VMEM((SUBLANES * PH, POOL_GW), F32),
                        pltpu.VMEM((SSD_STATE, D_INNER), F32)],
        compiler_params=_cparams("parallel", "arbitrary"),
        name="mixer_prompt",
    )(x, W["norm_mix8"], W["w_zxd"], W["w_pg"], W["conv_w8"], W["conv_b8"], W["dt_bias8"], W["a_neg8"],
      W["d_e8"], W["ssd_norm8"], W["w_ssd_proj"], W["pool_w"], W["pool_scale8"], W["w_o"])


def _attn_body(*refs, ssm_tb, n_alias):
    x_ref, g_ref, wq_ref, k_ref, v_ref, wo_ref, s_ref, xdt_ref, dec_ref, act_ref = refs[:10]
    o_ref, so_ref, y_ref = refs[10 + n_alias:]
    _attn_core(x_ref, g_ref, wq_ref, k_ref, v_ref, wo_ref, o_ref)
    _ssm_core(s_ref, xdt_ref, dec_ref, act_ref, so_ref, y_ref, ssm_tb)


def _attn_core(x_ref, g_ref, wq_ref, k_ref, v_ref, wo_ref, o_ref):
    x = x_ref[...]
    u = _rms(x, g_ref[...]).astype(BF16)
    q = (_dot(u, wq_ref[...]) * (X_HEAD_DIM ** -0.5)).astype(BF16)
    outs = []
    head = lambda ref, h: jnp.concatenate(
        [ref[pl.ds(half * X_HEADS + h, MEM_LEN, stride=SUBLANES), :] for half in range(HALVES)],
        axis=1).astype(BF16)
    for h in range(X_HEADS):
        s = _dot_nt(q[:, h * X_HEAD_DIM:(h + 1) * X_HEAD_DIM], head(k_ref, h))
        p = jnp.exp(s - jnp.max(s, axis=-1, keepdims=True))
        p = p / jnp.sum(p, axis=-1, keepdims=True)
        outs.append(_dot(p.astype(BF16), head(v_ref, h)))
    o = jnp.concatenate(outs, axis=1).astype(BF16)
    o_ref[...] = x + _dot(o, wo_ref[...])


def _attn_prompt(x, mk, mv, W, l, batch, seq, states, prev, xdt, dec, act, tm=512):
    tm = min(tm, seq)
    assert seq % tm == 0
    nt = seq // tm
    nsteps = batch * nt
    m = xdt.shape[0]
    assert m % nsteps == 0
    tb = m // nsteps
    tok = lambda b, i: (b * nt + i, 0)
    step3 = lambda b, i: (b * nt + i, 0, 0)
    kv = pl.BlockSpec((None, KV_ROWS, LANES), lambda b, i: (l, b, 0))
    cols = lambda v: v.reshape(nsteps, tb, D_INNER).transpose(0, 2, 1)
    st_spec = pl.BlockSpec((None, tb, D_INNER, SSD_STATE), lambda b, i: (l, b * nt + i, 0, 0))
    col_spec = pl.BlockSpec((1, D_INNER, tb), step3)
    in_specs = [pl.BlockSpec((tm, D_MODEL), tok),
                _layer_spec(l, (1, D_MODEL)),
                _layer_spec(l, (D_MODEL, D_MODEL), resident=True), kv, kv,
                _layer_spec(l, (D_MODEL, D_MODEL), resident=True),
                st_spec, col_spec, col_spec, pl.BlockSpec((1, tb, CONV_DIM), step3)]
    args = [x, W["norm_x"], W["w_xq"], mk, mv, W["w_xo"], states, cols(xdt), cols(dec),
            act.reshape(nsteps, tb, CONV_DIM)]
    aliases = {}
    if prev is not None:
        in_specs.append(pl.BlockSpec(memory_space=pl.ANY))
        args.append(prev)
        aliases = {len(args) - 1: 1}
    x, new_states, y = pl.pallas_call(
        functools.partial(_attn_body, ssm_tb=tb, n_alias=len(aliases)),
        grid=(batch, nt),
        in_specs=in_specs,
        out_specs=[pl.BlockSpec((tm, D_MODEL), tok), st_spec, pl.BlockSpec((1, tb, D_INNER), step3)],
        out_shape=[jax.ShapeDtypeStruct((batch * seq, D_MODEL), F32),
                   jax.ShapeDtypeStruct(states.shape, F32),
                   jax.ShapeDtypeStruct((nsteps, tb, D_INNER), F32)],
        input_output_aliases=aliases,
        compiler_params=_cparams("parallel", "parallel"),
        name="attn_prompt",
    )(*args)
    return x, new_states, y.reshape(m, D_INNER)


def _mlp_body(*refs, att_tb, final_norm):
    gf_ref = None
    if final_norm:
        gf_ref, refs = refs[4], refs[:4] + refs[5:]
    if att_tb:
        x_ref, g_ref, wu_ref, wd_ref, q_ref, k_ref, v_ref, o_ref, so_ref, xm_ref = refs
    else:
        x_ref, g_ref, wu_ref, wd_ref, o_ref, xm_ref = refs

    @pl.when(pl.program_id(1) == 0)
    def _():
        x = x_ref[...]
        xm_ref[...] = _rms(x, g_ref[...]).astype(BF16)
        o_ref[...] = x

    h = jnp.square(jnp.maximum(_dot(xm_ref[...], wu_ref[...]), 0.0)).astype(BF16)
    o_ref[...] += _dot(h, wd_ref[...])
    if att_tb:
        _sattn_core(q_ref, k_ref, v_ref, so_ref, att_tb)
    if final_norm:
        @pl.when(pl.program_id(1) == pl.num_programs(1) - 1)
        def _():
            o_ref[...] = _rms(o_ref[...], gf_ref[...])


def _mlp(x, W, l, side=None, final_g=None, tm=1024, tf=1024):
    m = x.shape[0]
    tm = min(tm, m)
    assert m % tm == 0
    nk = D_FF // tf
    in_specs = [pl.BlockSpec((tm, D_MODEL), lambda i, k: (i, 0)),
                _layer_spec(l, (1, D_MODEL)),
                pl.BlockSpec((None, D_MODEL, tf), lambda i, k: (l, 0, k)),
                pl.BlockSpec((None, tf, D_MODEL), lambda i, k: (l, k, 0))]
    out_specs = [pl.BlockSpec((tm, D_MODEL), lambda i, k: (i, 0))]
    out_shape = [jax.ShapeDtypeStruct((m, D_MODEL), F32)]
    args = [x, W["norm_mlp"], W["w_up"], W["w_down"]]
    if final_g is not None:
        in_specs.append(pl.BlockSpec((1, D_MODEL), lambda i, k: (0, 0)))
        args.append(final_g.reshape(1, D_MODEL))
    tb = 0
    if side is not None:
        q8, ck, cv = side
        ms = q8.shape[0]
        nsteps = (m // tm) * nk
        assert ms % nsteps == 0
        tb = ms // nsteps
        kv = pl.BlockSpec((None, tb, KV_ROWS, LANES), lambda i, k: (l, i * nk + k, 0, 0))
        qo = pl.BlockSpec((tb, SUBLANES, LANES), lambda i, k: (i * nk + k, 0, 0))
        in_specs += [qo, kv, kv]
        out_specs.append(qo)
        out_shape.append(jax.ShapeDtypeStruct((ms, SUBLANES, LANES), F32))
        args += [q8, ck, cv]
    outs = pl.pallas_call(
        functools.partial(_mlp_body, att_tb=tb, final_norm=final_g is not None),
        grid=(m // tm, nk),
        in_specs=in_specs,
        out_specs=out_specs,
        out_shape=out_shape,
        scratch_shapes=[pltpu.VMEM((tm, D_MODEL), BF16)],
        compiler_params=_cparams("parallel", "arbitrary"),
        name="mlp",
    )(*args)
    return outs[0] if side is None else outs


STEP_TB = 32


def _step_body(x_ref, g_ref, w_ref, wpg_ref, cst_ref, cw_ref, cb_ref, dtb_ref, a_ref, pst_ref, *rest, start_pos):
    ncst_ref, npst_ref, act_ref, xdt_ref, dec_ref, d_ref, z_ref, pg_ref = rest[-8:]
    x = x_ref[...]
    u = _mul8(x * lax.rsqrt(jnp.mean(x * x, axis=-1, keepdims=True) + EPS), g_ref[...]).astype(BF16)
    z_ref[...] = _dot(u, w_ref[:, 0:OFF_XBC])
    xbc = _dot(u, w_ref[:, OFF_XBC:OFF_DT])
    dtr = _dot(u, w_ref[:, OFF_DT:ZXD_W])
    pg = _dot(u, wpg_ref[...])
    pg_ref[...] = pg
    pu = pg[:, 0:D_MODEL]

    conv = None
    for k in range(CONV_WIDTH):
        src = cst_ref[k] if k < CONV_WIDTH - 1 else xbc
        term = _mul8(src, cw_ref[k])
        conv = _add8(term, cb_ref[...]) if conv is None else conv + term
    for k in range(CONV_WIDTH - 2):
        ncst_ref[k] = cst_ref[k + 1]
    ncst_ref[CONV_WIDTH - 2] = xbc
    act = _silu(conv)
    act_ref[...] = act

    dt = _softplus(_add8(dtr, dtb_ref[...]))
    dec = jnp.exp(_mul8(dt, a_ref[...]))
    lt64 = lax.broadcasted_iota(jnp.int32, (1, LANES), 1) < SSD_HEAD_DIM
    m = dt.shape[0]
    for k in range(D_INNER // LANES):
        dt_e = jnp.where(lt64, jnp.broadcast_to(dt[:, 2 * k:2 * k + 1], (m, LANES)),
                         jnp.broadcast_to(dt[:, 2 * k + 1:2 * k + 2], (m, LANES)))
        dec_e = jnp.where(lt64, jnp.broadcast_to(dec[:, 2 * k:2 * k + 1], (m, LANES)),
                          jnp.broadcast_to(dec[:, 2 * k + 1:2 * k + 2], (m, LANES)))
        xdt_ref[:, k * LANES:(k + 1) * LANES] = act[:, k * LANES:(k + 1) * LANES] * dt_e
        dec_ref[:, k * LANES:(k + 1) * LANES] = dec_e

    outs = []
    for gi, w in enumerate(POOL_WINDOWS):
        lo = gi * POOL_GW
        cur = pu[:, lo:lo + POOL_GW]
        acc = cur
        for k in range(1, w):
            acc = acc + pst_ref[POOL_BUF - k, :, lo:lo + POOL_GW]
        outs.append(acc / float(min(w, start_pos + 1)) - cur)
    d_ref[...] = jnp.concatenate(outs, axis=1).astype(d_ref.dtype)
    for j in range(POOL_BUF - 1):
        npst_ref[j] = pst_ref[j + 1]
    npst_ref[POOL_BUF - 1] = pu


def _sample_step(x, cst_t, pst_t, prev, W, l, start_pos):
    m = x.shape[0]
    tm = min(STEP_TB, m)
    rows = lambda width: pl.BlockSpec((tm, width), lambda i: (i, 0))
    state = lambda r, w: pl.BlockSpec((None, r, tm, w), lambda i: (l, 0, i, 0))
    widths = [CONV_DIM, D_INNER, D_INNER, D_MODEL, D_INNER, PG_W]
    dtypes = [F32, F32, F32, BF16, F32, F32]
    in_specs = [rows(D_MODEL), _layer_spec(l, (SUBLANES, D_MODEL)),
                _layer_spec(l, (D_MODEL, ZXD_W), resident=True), _layer_spec(l, (D_MODEL, PG_W), resident=True),
                state(CONV_WIDTH - 1, CONV_DIM), _layer_spec(l, (CONV_WIDTH, SUBLANES, CONV_DIM)),
                _layer_spec(l, (SUBLANES, CONV_DIM)), _layer_spec(l, (SUBLANES, LANES)),
                _layer_spec(l, (SUBLANES, LANES)), state(POOL_BUF, D_MODEL)]
    args = [x, W["norm_mix8"], W["w_zxd"], W["w_pg"], cst_t, W["conv_w8"], W["conv_b8"], W["dt_bias8"],
            W["a_neg8"], pst_t]
    aliases = {}
    if prev is not None:
        aliases = {len(args): 0, len(args) + 1: 1}
        in_specs += [pl.BlockSpec(memory_space=pl.ANY)] * 2
        args += list(prev)
    return pl.pallas_call(
        functools.partial(_step_body, start_pos=start_pos),
        grid=(m // tm,),
        in_specs=in_specs,
        out_specs=[state(CONV_WIDTH - 1, CONV_DIM), state(POOL_BUF, D_MODEL)] + [rows(w) for w in widths],
        out_shape=[jax.ShapeDtypeStruct(cst_t.shape, F32), jax.ShapeDtypeStruct(pst_t.shape, F32)]
        + [jax.ShapeDtypeStruct((m, w), dt) for w, dt in zip(widths, dtypes)],
        input_output_aliases=aliases,
        compiler_params=_cparams("parallel"),
        name="sample_step",
    )(*args)


def _ssm_core(s_ref, xdt_ref, dec_ref, act_ref, so_ref, y_ref, tb):
    for j in range(tb):
        for g in range(SSD_GROUPS):
            rows = slice(g * GROUP_W, (g + 1) * GROUP_W)
            b_row = act_ref[0, j:j + 1, D_INNER + g * SSD_STATE:D_INNER + (g + 1) * SSD_STATE]
            c_row = act_ref[0, j:j + 1, D_INNER + (SSD_GROUPS + g) * SSD_STATE:
                            D_INNER + (SSD_GROUPS + g + 1) * SSD_STATE]
            sn = s_ref[j, rows, :] * dec_ref[0, rows, j:j + 1] + xdt_ref[0, rows, j:j + 1] * b_row
            so_ref[j, rows, :] = sn
            c8 = jnp.broadcast_to(c_row, (SUBLANES, SSD_STATE)).astype(BF16)
            y_ref[0, j:j + 1, rows] = _dot_nt(c8, sn.astype(BF16))[0:1, :]


def _gate_body(y_ref, act_ref, z_ref, de_ref, nw_ref, o_ref):
    y = y_ref[...] + _mul8(act_ref[:, :D_INNER], de_ref[...])
    y = y * _silu(z_ref[...])
    o_ref[...] = _mul8(_group_rms(y), nw_ref[...]).astype(o_ref.dtype)


def _sample_gate(y, act, z, W, l):
    m = y.shape[0]
    full = lambda *shape: pl.BlockSpec(shape, lambda i: (0,) * len(shape))
    return pl.pallas_call(
        _gate_body,
        grid=(1,),
        in_specs=[full(m, D_INNER), full(m, CONV_DIM), full(m, D_INNER),
                  _layer_spec(l, (SUBLANES, D_INNER)), _layer_spec(l, (SUBLANES, D_INNER))],
        out_specs=full(m, D_INNER),
        out_shape=jax.ShapeDtypeStruct((m, D_INNER), BF16),
        compiler_params=_cparams("arbitrary"),
        name="sample_gate",
    )(y, act, z, W["d_e8"], W["ssd_norm8"])


def _mixs_body(ya_ref, d_ref, ga_ref, gb_ref, x_ref, wsp_ref, pw_ref, ps_ref, wo_ref, o_ref):
    o_ref[...] = x_ref[...] + _merge(ya_ref[...], d_ref[...], ga_ref[...], gb_ref[...],
                                     wsp_ref, pw_ref, lambda v: _mul8(v, ps_ref[...]), wo_ref)


def _mix_sample(ya, d, pg, x, W, l):
    m = x.shape[0]
    blk = lambda width, j: pl.BlockSpec((m, width), lambda i: (0, j))
    return pl.pallas_call(
        _mixs_body,
        grid=(1,),
        in_specs=[blk(D_INNER, 0), blk(D_MODEL, 0), blk(D_MODEL, 1), blk(D_MODEL, 2), blk(D_MODEL, 0),
                  _layer_spec(l, (D_INNER, D_MODEL)), _layer_spec(l, (len(POOL_WINDOWS), POOL_GW, POOL_GW)),
                  _layer_spec(l, (SUBLANES, D_MODEL)), _layer_spec(l, (D_MODEL, D_MODEL))],
        out_specs=blk(D_MODEL, 0),
        out_shape=jax.ShapeDtypeStruct((m, D_MODEL), F32),
        compiler_params=_cparams("arbitrary"),
        name="mix_sample",
    )(ya, d, pg, pg, x, W["w_ssd_proj"], W["pool_w"], W["pool_scale8"], W["w_o"])


KV_ROWS = MEM_LEN * SUBLANES
HALVES = X_HEAD_DIM // LANES
assert HALVES * X_HEADS == SUBLANES


def _sattn_core(q_ref, k_ref, v_ref, o_ref, tb):
    for j in range(tb):
        k3 = k_ref[j].reshape(MEM_LEN, SUBLANES, LANES)
        v3 = v_ref[j].reshape(MEM_LEN, SUBLANES, LANES)
        prod = k3 * q_ref[j][None]
        prod = prod + pltpu.roll(prod, X_HEADS, axis=1)
        s = jnp.sum(prod, axis=-1, keepdims=True)
        e = jnp.exp(s - jnp.max(s, axis=0, keepdims=True))
        pr = e / jnp.sum(e, axis=0, keepdims=True)
        o_ref[j] = jnp.sum(pr * v3, axis=0)


def _kv_rows(c):
    depth, m = c.shape[:2]
    c = c.reshape(depth, m, MEM_LEN, X_HEADS, HALVES, LANES).transpose(0, 1, 2, 4, 3, 5)
    return c.reshape(depth, m, KV_ROWS, LANES)


def _kv_unrows(r, batch):
    depth = r.shape[0]
    r = r.reshape(depth, batch, MEM_LEN, HALVES, X_HEADS, LANES).transpose(0, 1, 2, 4, 3, 5)
    return r.reshape(depth, batch, MEM_LEN, X_HEADS, X_HEAD_DIM)


def _q_rows(q):
    m = q.shape[0]
    return q.reshape(m, X_HEADS, HALVES, LANES).transpose(0, 2, 1, 3).reshape(m, SUBLANES, LANES)


def _o_cols(o8):
    m = o8.shape[0]
    return o8.reshape(m, HALVES, X_HEADS, LANES).transpose(0, 2, 1, 3).reshape(m, D_MODEL)


def _prep_weights(norm_mix, w_in, conv_w, conv_b, dt_bias, a_log, d_skip, ssd_norm, w_ssd_proj, pool_w,
                  pool_scale, w_o, norm_x, w_xq, w_xk, w_xv, w_xo, norm_mlp, w_up, w_down):
    depth = w_in.shape[0]
    lane_pad = lambda v: jnp.concatenate([v, jnp.zeros((depth, LANES - SSD_HEADS), F32)], axis=1)
    rep8 = lambda v: jnp.broadcast_to(v[:, None, :], (depth, SUBLANES, v.shape[-1]))
    row = lambda v: v[:, None, :]
    return dict(
        norm_mix8=rep8(norm_mix),
        w_zxd=w_in[:, :, :ZXD_W].astype(BF16),
        w_pg=w_in[:, :, OFF_POOL:].astype(BF16),
        conv_w8=jnp.broadcast_to(conv_w[:, :, None, :], (depth, CONV_WIDTH, SUBLANES, CONV_DIM)),
        conv_b8=rep8(conv_b),
        dt_bias8=rep8(lane_pad(dt_bias)),
        a_neg8=rep8(lane_pad(-jnp.exp(a_log))),
        d_e8=rep8(jnp.repeat(d_skip, SSD_HEAD_DIM, axis=1)),
        ssd_norm8=rep8(ssd_norm),
        w_ssd_proj=w_ssd_proj.astype(BF16),
        pool_w=pool_w.astype(BF16), pool_scale8=rep8(pool_scale),
        w_o=w_o.astype(BF16), norm_x=row(norm_x), w_xq=w_xq.astype(BF16),
        w_xk=w_xk.astype(BF16), w_xv=w_xv.astype(BF16), w_xo=w_xo.astype(BF16),
        norm_mlp=row(norm_mlp), w_up=w_up.astype(BF16), w_down=w_down.astype(BF16))


def _layer(hp, hs, mk, mv, states, new_states, cst_t, pst_t, new_hist, ck, cv, W, l, batch, seq, final_g):
    ncst, npst, act, xdt, dec, d, z, pg = _sample_step(hs, cst_t, pst_t, new_hist, W, l, PAST_LEN)
    hp, sst, cst, pst = _mixer_prompt(hp, W, l, batch, seq)
    hp, new_states, y = _attn_prompt(hp, mk, mv, W, l, batch, seq, states, new_states, xdt, dec, act)
    ya = _sample_gate(y, act, z, W, l)
    hs = _mix_sample(ya, d, pg, hs, W, l)
    q = _mm(hs, W["w_xq"], l, g=W["norm_x"], out_scale=X_HEAD_DIM ** -0.5)
    hp, o8 = _mlp(hp, W, l, side=(_q_rows(q), ck, cv), final_g=final_g)
    hs = _mm(_o_cols(o8), W["w_xo"], l, res=hs)
    hs = _mlp(hs, W, l, final_g=final_g)
    return hp, hs, new_states, (ncst, npst), (sst, cst, pst)


def kernel(x_prompt, x_sample, mem_prompt, state_ssm, state_conv, state_pool, cache_mem_k, cache_mem_v,
           norm_mix, w_in, conv_w, conv_b, dt_bias, a_log, d_skip, ssd_norm, w_ssd_proj, pool_w, pool_scale,
           w_o, norm_x, w_xq, w_xk, w_xv, w_xo, norm_mlp, w_up, w_down, norm_final):
    bp, seq, _ = x_prompt.shape
    bs = x_sample.shape[0]
    depth = w_in.shape[0]
    assert x_sample.shape[1] == 1 and seq % CHUNK == 0
    W = _prep_weights(norm_mix, w_in, conv_w, conv_b, dt_bias, a_log, d_skip, ssd_norm, w_ssd_proj, pool_w,
                      pool_scale, w_o, norm_x, w_xq, w_xk, w_xv, w_xo, norm_mlp, w_up, w_down)
    hp = x_prompt.reshape(bp * seq, D_MODEL)
    hs = x_sample.reshape(bs, D_MODEL)
    mem = mem_prompt.reshape(bp * MEM_LEN, D_MODEL)
    mk, mv = _kv_proj(mem, W["w_xk"], W["w_xv"])
    states = state_ssm.reshape(depth, bs, D_INNER, SSD_STATE)
    ck = _kv_rows(cache_mem_k)
    cv = _kv_rows(cache_mem_v)
    by_row = lambda v: v.transpose(0, 2, 1, 3)
    cst_t, pst_t = by_row(state_conv), by_row(state_pool)
    new_states = new_hist = None
    outs = [[] for _ in range(3)]
    for l in range(depth):
        hp, hs, new_states, new_hist, (sst, cst, pst) = _layer(
            hp, hs, mk, mv, states, new_states, cst_t, pst_t, new_hist, ck, cv, W, l, bp, seq,
            norm_final if l == depth - 1 else None)
        outs[0].append(sst.reshape(bp, SSD_HEADS, SSD_HEAD_DIM, SSD_STATE))
        outs[1].append(cst)
        outs[2].append(pst)
    y_prompt = hp.reshape(bp, seq, D_MODEL)
    y_sample = hs.reshape(bs, 1, D_MODEL)
    ssm_p, conv_p, pool_p = (jnp.stack(o) for o in outs)
    return (y_prompt, y_sample, ssm_p, conv_p, pool_p, _kv_unrows(mk, bp), _kv_unrows(mv, bp),
            new_states.reshape(depth, bs, SSD_HEADS, SSD_HEAD_DIM, SSD_STATE),
            by_row(new_hist[0]), by_row(new_hist[1]))
```
